```python
import math
import jax, jax.numpy as jnp
from jax import lax
import numpy as np

D_MODEL = 1024
BATCH = 4
SEQ = 8192
DEPTH = 4

N_A = DEPTH // 2
N_B = DEPTH - N_A
EPS = 1e-6

D_FF = ((8 * D_MODEL + 3 * 256 - 1) // (3 * 256)) * 256

SSM_EXPAND = 2
D_INNER = SSM_EXPAND * D_MODEL
SSM_HEADDIM = 64
SSM_HEADS = D_INNER // SSM_HEADDIM
SSM_STATE = 128
SSM_GROUPS = 4
SSM_HPG = SSM_HEADS // SSM_GROUPS
CONV_W = 4
SSD_CHUNK = 256
D_XBC = D_INNER + 2 * SSM_GROUPS * SSM_STATE
D_SSM_IN = D_INNER + D_XBC + SSM_HEADS

NSA_HEADS = 16
NSA_KV_HEADS = 4
NSA_GROUP = NSA_HEADS // NSA_KV_HEADS
NSA_HEAD_DIM = 64
CMP_BLOCK = 32
CMP_STRIDE = 16
CMP_HIDDEN = 256
SEL_BLOCK = 64
N_SEL = 16
WINDOW = 512
Q_BLOCK = 128
N_BRANCH = 3
D_NSA_Q = NSA_HEADS * NSA_HEAD_DIM + NSA_HEADS * N_BRANCH
D_KV = 6 * NSA_KV_HEADS * NSA_HEAD_DIM

ROPE_THETA = 500000.0
ROT_DIMS = NSA_HEAD_DIM // 4

kernel_name = "yoco_mamba2_nsa_hybrid"


def rmsnorm(x, g):
    xf = x.astype(jnp.float32)
    y = xf * lax.rsqrt(jnp.mean(xf * xf, axis=-1, keepdims=True) + EPS)
    return (y * g.astype(jnp.float32)).astype(x.dtype)


def adaln(x, g, shift, scale):
    return rmsnorm(x, g) * (1 + scale[:, None, :]) + shift[:, None, :]


def partial_rope(x, pos):
    half = ROT_DIMS // 2
    inv_freq = ROPE_THETA ** (-jnp.arange(half, dtype=jnp.float32) / half)
    ang = pos.astype(jnp.float32)[:, None] * inv_freq[None, :]
    bshape = (1, pos.shape[0]) + (1,) * (x.ndim - 3) + (half,)
    cos = jnp.cos(ang).reshape(bshape)
    sin = jnp.sin(ang).reshape(bshape)
    x1 = x[..., :half].astype(jnp.float32)
    x2 = x[..., half:ROT_DIMS].astype(jnp.float32)
    rot = jnp.concatenate([x1 * cos - x2 * sin, x2 * cos + x1 * sin], axis=-1).astype(x.dtype)
    return jnp.concatenate([rot, x[..., ROT_DIMS:]], axis=-1)


def masked_softmax(s, mask, axis):
    s = jnp.where(mask, s.astype(jnp.float32), -jnp.inf)
    m = jnp.max(s, axis=axis, keepdims=True)
    m = jnp.where(jnp.isfinite(m), m, 0.0)
    p = jnp.exp(s - m)
    d = jnp.sum(p, axis=axis, keepdims=True)
    return p / jnp.where(d > 0, d, 1.0)


def swiglu(h, w_in, w_out):
    a, b = jnp.split(h @ w_in, 2, axis=-1)
    return (jax.nn.silu(a) * b) @ w_out


def causal_dwconv(x, w, b):
    y = lax.conv_general_dilated(
        x, w[:, None, :].astype(x.dtype), window_strides=(1,),
        padding=[(CONV_W - 1, 0)], dimension_numbers=("NWC", "WIO", "NWC"),
        feature_group_count=x.shape[-1])
    return y + b


def ssd_scan(x, bm, cm, dt_raw, dt_bias, a_log):
    f32 = jnp.float32
    bsz, s_len, n_h, p_dim = x.shape
    L = math.gcd(s_len, SSD_CHUNK)
    nc = s_len // L
    G, J, N = SSM_GROUPS, SSM_HPG, SSM_STATE
    A = -jnp.exp(a_log.astype(f32))
    dt = jax.nn.softplus(dt_raw.astype(f32) + dt_bias.astype(f32))

    def to_chunks(a):
        return jnp.moveaxis(a.astype(f32).reshape((bsz, nc, L) + a.shape[2:]), 1, 0)

    causal = jnp.tril(jnp.ones((L, L), dtype=bool))

    def step(state, inp):
        xk, bk, ck, dtk = inp
        acs = jnp.cumsum(dtk * A, axis=1)
        seg = acs[:, :, None, :] - acs[:, None, :, :]
        decay = jnp.exp(jnp.where(causal[None, :, :, None], seg, -jnp.inf)).reshape(bsz, L, L, G, J)
        xdt = (xk * dtk[..., None]).reshape(bsz, L, G, J, p_dim)
        cb = jnp.einsum("blgn,bsgn->blsg", ck, bk)
        y_intra = jnp.einsum("blsg,blsgj,bsgjp->blgjp", cb, decay, xdt)
        y_inter = jnp.einsum("blgn,bgjpn->blgjp", ck, state) * jnp.exp(acs).reshape(bsz, L, G, J)[..., None]
        tail = jnp.exp(acs[:, -1:, :] - acs).reshape(bsz, L, G, J)
        state = (state * jnp.exp(acs[:, -1]).reshape(bsz, G, J)[..., None, None]
                 + jnp.einsum("bsgn,bsgj,bsgjp->bgjpn", bk, tail, xdt))
        return state, (y_intra + y_inter).reshape(bsz, L, n_h, p_dim)

    state0 = jnp.zeros((bsz, G, J, p_dim, N), f32)
    _, ys = lax.scan(step, state0, (to_chunks(x), to_chunks(bm), to_chunks(cm), to_chunks(dt)))
    return jnp.moveaxis(ys, 0, 1).reshape(bsz, s_len, n_h, p_dim)


def mamba2_mixer(h, w_in, conv_w, conv_b, dt_bias, a_log, d_skip, norm_g, w_out):
    bsz, s_len, _ = h.shape
    proj = h @ w_in
    z, xbc, dt = jnp.split(proj, [D_INNER, D_INNER + D_XBC], axis=-1)
    xbc = jax.nn.silu(causal_dwconv(xbc, conv_w, conv_b))
    xs, bm, cm = jnp.split(xbc, [D_INNER, D_INNER + SSM_GROUPS * SSM_STATE], axis=-1)
    xs = xs.reshape(bsz, s_len, SSM_HEADS, SSM_HEADDIM)
    y = ssd_scan(xs, bm.reshape(bsz, s_len, SSM_GROUPS, SSM_STATE),
                 cm.reshape(bsz, s_len, SSM_GROUPS, SSM_STATE), dt, dt_bias, a_log)
    y = (y + d_skip.astype(jnp.float32)[:, None] * xs.astype(jnp.float32)).astype(h.dtype)
    y = y.reshape(bsz, s_len, D_INNER) * jax.nn.silu(z)
    y = rmsnorm(y.reshape(bsz, s_len, SSM_GROUPS, D_INNER // SSM_GROUPS),
                norm_g.reshape(SSM_GROUPS, D_INNER // SSM_GROUPS))
    return y.reshape(bsz, s_len, D_INNER) @ w_out


def nsa_shared_kv(stream, cs, kv_norm, kv_mod_w, kv_mod_b, w_kv,
                  cmp_pos_k, cmp_w1_k, cmp_w2_k, cmp_pos_v, cmp_w1_v, cmp_w2_v):
    bsz, s_len, _ = stream.shape
    shift, scale = jnp.split(cs @ kv_mod_w + kv_mod_b, 2, axis=-1)
    h = adaln(stream, kv_norm, shift, scale)
    kv = (h @ w_kv).reshape(bsz, s_len, 6, NSA_KV_HEADS, NSA_HEAD_DIM)
    k_c, v_c, k_s, v_s, k_w, v_w = [kv[:, :, i] for i in range(6)]
    pos = jnp.arange(s_len)
    n_cmp = (s_len - CMP_BLOCK) // CMP_STRIDE + 1
    idx = jnp.arange(n_cmp)[:, None] * CMP_STRIDE + jnp.arange(CMP_BLOCK)[None, :]

    def compress(t, pe, w1, w2):
        blk = t[:, idx] + pe[None, None, :, None, :]
        blk = jnp.moveaxis(blk, 3, 2).reshape(bsz, n_cmp, NSA_KV_HEADS, CMP_BLOCK * NSA_HEAD_DIM)
        return jax.nn.silu(blk @ w1) @ w2

    k_cmp = compress(k_c, cmp_pos_k, cmp_w1_k, cmp_w2_k)
    v_cmp = compress(v_c, cmp_pos_v, cmp_w1_v, cmp_w2_v)
    nsb = s_len // SEL_BLOCK
    k_sel = partial_rope(k_s, pos).reshape(bsz, nsb, SEL_BLOCK, NSA_KV_HEADS, NSA_HEAD_DIM).transpose(0, 3, 1, 2, 4)
    v_sel = v_s.reshape(bsz, nsb, SEL_BLOCK, NSA_KV_HEADS, NSA_HEAD_DIM).transpose(0, 3, 1, 2, 4)
    pad = ((0, 0), (WINDOW, 0), (0, 0), (0, 0))
    k_win = jnp.pad(partial_rope(k_w, pos), pad)
    v_win = jnp.pad(v_w, pad)
    return k_cmp, v_cmp, k_sel, v_sel, k_win, v_win


def nsa_mixer(h, w_q, w_o, k_cmp, v_cmp, k_sel, v_sel, k_win, v_win):
    bsz, s_len, _ = h.shape
    nq = s_len // Q_BLOCK
    nsb = s_len // SEL_BLOCK
    n_sel = min(N_SEL, nsb)
    n_cmp = k_cmp.shape[1]
    n_units = s_len // CMP_STRIDE
    units = CMP_BLOCK // CMP_STRIDE
    scale = NSA_HEAD_DIM ** -0.5
    proj = h @ w_q
    q = proj[..., :NSA_HEADS * NSA_HEAD_DIM].reshape(bsz, s_len, NSA_KV_HEADS, NSA_GROUP, NSA_HEAD_DIM)
    gates = jax.nn.sigmoid(proj[..., NSA_HEADS * NSA_HEAD_DIM:].astype(jnp.float32)).reshape(
        bsz, s_len, NSA_KV_HEADS, NSA_GROUP, N_BRANCH)
    q_rot = partial_rope(q, jnp.arange(s_len))
    cmp_end = jnp.arange(n_cmp) * CMP_STRIDE + CMP_BLOCK - 1

    def blocks(a):
        return jnp.moveaxis(a.reshape((bsz, nq, Q_BLOCK) + a.shape[2:]), 1, 0)

    def one_block(args):
        qi, qc, qr, g = args
        t = qi * Q_BLOCK + jnp.arange(Q_BLOCK)
        s = jnp.einsum("bqhgd,bchd->bhgqc", qc, k_cmp) * scale
        p_cmp = masked_softmax(s, (cmp_end[None, :] <= t[:, None])[None, None, None], axis=-1)
        o_cmp = jnp.einsum("bhgqc,bchd->bqhgd", p_cmp.astype(v_cmp.dtype), v_cmp)
        p_grp = p_cmp.sum(axis=2)
        u = sum(jnp.pad(p_grp, ((0, 0), (0, 0), (0, 0), (r, n_units - n_cmp - r))) for r in range(units))
        imp = u.reshape(bsz, NSA_KV_HEADS, Q_BLOCK, nsb, SEL_BLOCK // CMP_STRIDE).sum(axis=-1)
        qblk = t // SEL_BLOCK
        j = jnp.arange(nsb)
        forced = (j[None] == 0) | (j[None] == qblk[:, None]) | (j[None] == qblk[:, None] - 1)
        causal_blk = j[None] <= qblk[:, None]
        score = jnp.where(forced, jnp.inf, jnp.where(causal_blk, imp, -1.0))
        _, sel = lax.top_k(score, n_sel)
        gather = jax.vmap(jax.vmap(lambda kk, ii: kk[ii]))
        ks = gather(k_sel, sel)
        vs = gather(v_sel, sel)
        tok = sel[..., None] * SEL_BLOCK + jnp.arange(SEL_BLOCK)
        s = jnp.einsum("bqhgd,bhqnkd->bhgqnk", qr, ks) * scale
        p = masked_softmax(s, (tok <= t[None, None, :, None, None])[:, :, None], axis=(-2, -1))
        o_sel = jnp.einsum("bhgqnk,bhqnkd->bqhgd", p.astype(vs.dtype), vs)
        kw = lax.dynamic_slice_in_dim(k_win, qi * Q_BLOCK, WINDOW + Q_BLOCK, axis=1)
        vw = lax.dynamic_slice_in_dim(v_win, qi * Q_BLOCK, WINDOW + Q_BLOCK, axis=1)
        kp = qi * Q_BLOCK - WINDOW + jnp.arange(WINDOW + Q_BLOCK)
        wmask = (kp[None] <= t[:, None]) & (kp[None] > t[:, None] - WINDOW) & (kp[None] >= 0)
        s = jnp.einsum("bqhgd,bkhd->bhgqk", qr, kw) * scale
        p = masked_softmax(s, wmask[None, None, None], axis=-1)
        o_win = jnp.einsum("bhgqk,bkhd->bqhgd", p.astype(vw.dtype), vw)
        o = g[..., 0:1] * o_cmp + g[..., 1:2] * o_sel + g[..., 2:3] * o_win
        return o.astype(h.dtype)

    o = lax.map(one_block, (jnp.arange(nq), blocks(q), blocks(q_rot), blocks(gates)))
    o = jnp.moveaxis(o, 0, 1).reshape(bsz, s_len, NSA_HEADS * NSA_HEAD_DIM)
    return o @ w_o


def setup_inputs(seed: int = 0) -> dict:
    key = jax.random.key(seed)
    ks = iter(jax.random.split(key, 48))
    f32 = jnp.float32
    D = D_MODEL

    def nrm(shape, scale):
        return jax.random.normal(next(ks), shape, f32) * scale

    def gain(shape):
        return 1.0 + nrm(shape, 0.05)

    x = nrm((BATCH, SEQ, D), 1.0)
    c = nrm((BATCH, D), 1.0)
    gate_offset = jnp.tile(jnp.repeat(jnp.array([0.0, 0.0, 1.0], f32), D), 2)
    mod_w = nrm((DEPTH, D, 6 * D), 0.1 * D ** -0.5)
    mod_b = nrm((DEPTH, 6 * D), 0.02) + gate_offset
    norm_pre_mix = gain((DEPTH, D))
    norm_post_mix = gain((DEPTH, D))
    norm_pre_ffn = gain((DEPTH, D))
    norm_post_ffn = gain((DEPTH, D))
    ffn_w_in = nrm((DEPTH, D, 2 * D_FF), D ** -0.5)
    ffn_w_out = nrm((DEPTH, D_FF, D), D_FF ** -0.5)
    ssm_w_in = nrm((N_A, D, D_SSM_IN), D ** -0.5)
    ssm_conv_w = nrm((N_A, CONV_W, D_XBC), CONV_W ** -0.5)
    ssm_conv_b = nrm((N_A, D_XBC), 0.02)
    dt0 = jnp.exp(jax.random.uniform(next(ks), (N_A, SSM_HEADS), f32, math.log(1e-3), math.log(1e-1)))
    ssm_dt_bias = dt0 + jnp.log(-jnp.expm1(-dt0))
    ssm_a_log = jnp.log(jax.random.uniform(next(ks), (N_A, SSM_HEADS), f32, 1.0, 16.0))
    ssm_d = gain((N_A, SSM_HEADS))
    ssm_norm = gain((N_A, D_INNER))
    ssm_w_out = nrm((N_A, D_INNER, D), D_INNER ** -0.5)
    kv_norm = gain((D,))
    kv_mod_w = nrm((D, 2 * D), 0.1 * D ** -0.5)
    kv_mod_b = nrm((2 * D,), 0.02)
    w_kv = nrm((D, D_KV), D ** -0.5)
    cmp_pos_k = nrm((CMP_BLOCK, NSA_HEAD_DIM), 0.1)
    cmp_w1_k = nrm((CMP_BLOCK * NSA_HEAD_DIM, CMP_HIDDEN), (CMP_BLOCK * NSA_HEAD_DIM) ** -0.5)
    cmp_w2_k = nrm((CMP_HIDDEN, NSA_HEAD_DIM), CMP_HIDDEN ** -0.5)
    cmp_pos_v = nrm((CMP_BLOCK, NSA_HEAD_DIM), 0.1)
    cmp_w1_v = nrm((CMP_BLOCK * NSA_HEAD_DIM, CMP_HIDDEN), (CMP_BLOCK * NSA_HEAD_DIM) ** -0.5)
    cmp_w2_v = nrm((CMP_HIDDEN, NSA_HEAD_DIM), CMP_HIDDEN ** -0.5)
    nsa_w_q = nrm((N_B, D, D_NSA_Q), D ** -0.5)
    nsa_w_o = nrm((N_B, NSA_HEADS * NSA_HEAD_DIM, D), (NSA_HEADS * NSA_HEAD_DIM) ** -0.5)
    return {"x": x, "c": c, "mod_w": mod_w, "mod_b": mod_b,
            "norm_pre_mix": norm_pre_mix, "norm_post_mix": norm_post_mix,
            "norm_pre_ffn": norm_pre_ffn, "norm_post_ffn": norm_post_ffn,
            "ffn_w_in": ffn_w_in, "ffn_w_out": ffn_w_out,
            "ssm_w_in": ssm_w_in, "ssm_conv_w": ssm_conv_w, "ssm_conv_b": ssm_conv_b,
            "ssm_dt_bias": ssm_dt_bias, "ssm_a_log": ssm_a_log, "ssm_d": ssm_d,
            "ssm_norm": ssm_norm, "ssm_w_out": ssm_w_out,
            "kv_norm": kv_norm, "kv_mod_w": kv_mod_w, "kv_mod_b": kv_mod_b, "w_kv": w_kv,
            "cmp_pos_k": cmp_pos_k, "cmp_w1_k": cmp_w1_k, "cmp_w2_k": cmp_w2_k,
            "cmp_pos_v": cmp_pos_v, "cmp_w1_v": cmp_w1_v, "cmp_w2_v": cmp_w2_v,
            "nsa_w_q": nsa_w_q, "nsa_w_o": nsa_w_o}


def reference(x, c, mod_w, mod_b, norm_pre_mix, norm_post_mix, norm_pre_ffn, norm_post_ffn,
              ffn_w_in, ffn_w_out, ssm_w_in, ssm_conv_w, ssm_conv_b, ssm_dt_bias, ssm_a_log,
              ssm_d, ssm_norm, ssm_w_out, kv_norm, kv_mod_w, kv_mod_b, w_kv,
              cmp_pos_k, cmp_w1_k, cmp_w2_k, cmp_pos_v, cmp_w1_v, cmp_w2_v, nsa_w_q, nsa_w_o):
    cs = jax.nn.silu(c)
    shared = None
    for i in range(DEPTH):
        sh_m, sc_m, g_m, sh_f, sc_f, g_f = jnp.split(cs @ mod_w[i] + mod_b[i], 6, axis=-1)
        if i == N_A:
            shared = nsa_shared_kv(x, cs, kv_norm, kv_mod_w, kv_mod_b, w_kv,
                                   cmp_pos_k, cmp_w1_k, cmp_w2_k, cmp_pos_v, cmp_w1_v, cmp_w2_v)
        h = adaln(x, norm_pre_mix[i], sh_m, sc_m)
        if i < N_A:
            y = mamba2_mixer(h, ssm_w_in[i], ssm_conv_w[i], ssm_conv_b[i], ssm_dt_bias[i],
                             ssm_a_log[i], ssm_d[i], ssm_norm[i], ssm_w_out[i])
        else:
            y = nsa_mixer(h, nsa_w_q[i - N_A], nsa_w_o[i - N_A], *shared)
        x = x + g_m[:, None, :] * rmsnorm(y, norm_post_mix[i])
        h = adaln(x, norm_pre_ffn[i], sh_f, sc_f)
        x = x + g_f[:, None, :] * rmsnorm(swiglu(h, ffn_w_in[i], ffn_w_out[i]), norm_post_ffn[i])
    return x
```

```python
import functools
import math

import numpy as np
import jax
import jax.numpy as jnp
from jax import lax
from jax.experimental import pallas as pl
from jax.experimental.pallas import tpu as pltpu

F32 = jnp.float32
BF16 = jnp.bfloat16

EPS = 1e-6
DEPTH = 4
N_A = DEPTH // 2

SSM_HEADDIM = 64
SSM_STATE = 128
SSM_GROUPS = 4
CONV_W = 4
SSD_CHUNK = 256

NSA_HEADS = 16
NSA_KV_HEADS = 4
NSA_GROUP = NSA_HEADS // NSA_KV_HEADS
NSA_HEAD_DIM = 64
CMP_BLOCK = 32
CMP_STRIDE = 16
SEL_BLOCK = 64
N_SEL = 16
WINDOW = 512
Q_BLOCK = 128
N_BRANCH = 3
ROPE_THETA = 500000.0
ROT_DIMS = NSA_HEAD_DIM // 4

LANES = 128
KEY_CHUNK = 512
WIN_UNITS = WINDOW // Q_BLOCK + 1
MASK_BIAS = float(2 ** 30)
VMEM_LIMIT = 56 * 1024 * 1024


def _params(*sem):
    return pltpu.CompilerParams(dimension_semantics=sem, vmem_limit_bytes=VMEM_LIMIT)


def _rms(x, g):
    return x * lax.rsqrt(jnp.mean(x * x, axis=-1, keepdims=True) + EPS) * g


def _silu(x):
    return x * jax.nn.sigmoid(x)


def _mods_kernel(c_ref, w_ref, b_ref, o_ref):
    cs = _silu(c_ref[...])
    o_ref[...] = jnp.dot(cs, w_ref[...], preferred_element_type=F32) + b_ref[...]


def _mods(c, w, b):
    n_l, d, n = w.shape
    bsz = c.shape[0]
    tn = 1024
    return pl.pallas_call(
        _mods_kernel,
        grid=(n_l, n // tn),
        in_specs=[pl.BlockSpec((bsz, d), lambda l, j: (0, 0)),
                  pl.BlockSpec((None, d, tn), lambda l, j: (l, 0, j)),
                  pl.BlockSpec((None, 1, tn), lambda l, j: (l, 0, j))],
        out_specs=pl.BlockSpec((None, bsz, tn), lambda l, j: (l, 0, j)),
        out_shape=jax.ShapeDtypeStruct((n_l, bsz, n), F32),
        compiler_params=_params("parallel", "parallel"),
        name="mods",
    )(c, w, b.reshape(n_l, 1, n))


def _adaln_mm_kernel(x_ref, g_ref, sh_ref, sc_ref, *refs, n_w):
    h = _rms(x_ref[...], g_ref[...]) * (1.0 + sc_ref[...]) + sh_ref[...]
    hb = h.astype(BF16)
    for w_ref, o_ref in zip(refs[:n_w], refs[n_w:]):
        o_ref[...] = jnp.dot(hb, w_ref[...], preferred_element_type=F32).astype(o_ref.dtype)


def _adaln_mm(x, g, shift, scale, ws, tm=256):
    bsz, s_len, d = x.shape
    n_w = len(ws)
    const = lambda b, i: (0, 0)
    return pl.pallas_call(
        functools.partial(_adaln_mm_kernel, n_w=n_w),
        grid=(bsz, s_len // tm),
        in_specs=[pl.BlockSpec((None, tm, d), lambda b, i: (b, i, 0)),
                  pl.BlockSpec((1, d), const),
                  pl.BlockSpec((None, 1, d), lambda b, i: (b, 0, 0)),
                  pl.BlockSpec((None, 1, d), lambda b, i: (b, 0, 0))]
                 + [pl.BlockSpec(w.shape, const) for w in ws],
        out_specs=[pl.BlockSpec((None, tm, w.shape[1]), lambda b, i: (b, i, 0)) for w in ws],
        out_shape=[jax.ShapeDtypeStruct((bsz, s_len, w.shape[1]), F32) for w in ws],
        compiler_params=_params("parallel", "parallel"),
        name="adaln_mm",
    )(x, g.reshape(1, d), shift, scale, *ws)


def _mm_post_kernel(a_ref, w_ref, x_ref, gate_ref, g_ref, o_ref):
    y = jnp.dot(a_ref[...], w_ref[...], preferred_element_type=F32)
    o_ref[...] = x_ref[...] + gate_ref[...] * _rms(y, g_ref[...])


def _mm_post(a, w, x, gate, g, tm=512):
    bsz, s_len, d = x.shape
    k = a.shape[-1]
    return pl.pallas_call(
        _mm_post_kernel,
        grid=(bsz, s_len // tm),
        in_specs=[pl.BlockSpec((None, tm, k), lambda b, i: (b, i, 0)),
                  pl.BlockSpec((k, d), lambda b, i: (0, 0)),
                  pl.BlockSpec((None, tm, d), lambda b, i: (b, i, 0)),
                  pl.BlockSpec((None, 1, d), lambda b, i: (b, 0, 0)),
                  pl.BlockSpec((1, d), lambda b, i: (0, 0))],
        out_specs=pl.BlockSpec((None, tm, d), lambda b, i: (b, i, 0)),
        out_shape=jax.ShapeDtypeStruct(x.shape, F32),
        compiler_params=_params("parallel", "parallel"),
        name="mm_post",
    )(a, w, x, gate, g.reshape(1, d))


def _ffn_kernel(x_ref, g1_ref, sh_ref, sc_ref, wa_ref, wb_ref, wo_ref, g2_ref, gate_ref, o_ref, u_ref, *, tc):
    x = x_ref[...]
    hb = (_rms(x, g1_ref[...]) * (1.0 + sc_ref[...]) + sh_ref[...]).astype(BF16)
    d_ff = wa_ref.shape[1]
    for c0 in range(0, d_ff, tc):
        a = jnp.dot(hb, wa_ref[:, c0:c0 + tc], preferred_element_type=F32)
        b = jnp.dot(hb, wb_ref[:, c0:c0 + tc], preferred_element_type=F32)
        u_ref[:, c0:c0 + tc] = (_silu(a) * b).astype(BF16)
    y = jnp.dot(u_ref[...], wo_ref[...], preferred_element_type=F32)
    o_ref[...] = x + gate_ref[...] * _rms(y, g2_ref[...])


def _ffn(x, g1, shift, scale, wa, wb, wo, g2, gate, tm=512, tc=256):
    bsz, s_len, d = x.shape
    d_ff = wa.shape[1]
    const = lambda b, i: (0, 0)
    vec = lambda b, i: (b, 0, 0)
    return pl.pallas_call(
        functools.partial(_ffn_kernel, tc=tc),
        grid=(bsz, s_len // tm),
        in_specs=[pl.BlockSpec((None, tm, d), lambda b, i: (b, i, 0)),
                  pl.BlockSpec((1, d), const),
                  pl.BlockSpec((None, 1, d), vec),
                  pl.BlockSpec((None, 1, d), vec),
                  pl.BlockSpec((d, d_ff), const),
                  pl.BlockSpec((d, d_ff), const),
                  pl.BlockSpec((d_ff, d), const),
                  pl.BlockSpec((1, d), const),
                  pl.BlockSpec((None, 1, d), vec)],
        out_specs=pl.BlockSpec((None, tm, d), lambda b, i: (b, i, 0)),
        out_shape=jax.ShapeDtypeStruct(x.shape, F32),
        scratch_shapes=[pltpu.VMEM((tm, d_ff), BF16)],
        compiler_params=_params("parallel", "parallel"),
        name="ffn",
    )(x, g1.reshape(1, d), shift, scale, wa, wb, wo, g2.reshape(1, d), gate)


def _pair(mat, p, lane_lo):
    return jnp.where(lane_lo, mat[:, 2 * p:2 * p + 1], mat[:, 2 * p + 1:2 * p + 2])


def _ssd_kernel(xbc_ref, z_ref, dt_ref, cw_ref, cb_ref, dtb_ref, alog_ref, dsk_ref, ng_ref, o_ref,
                ext_ref, xs_ref, state_ref, y_ref, *, d_inner, n_groups):
    L = SSD_CHUNK
    N = SSM_STATE
    gw = d_inner // n_groups
    pairs_per_group = gw // LANES
    n_pairs = d_inner // LANES
    c_idx = pl.program_id(1)

    @pl.when(c_idx == 0)
    def _():
        ext_ref[0:8, :] = jnp.zeros((8, ext_ref.shape[1]), F32)
        state_ref[...] = jnp.zeros(state_ref.shape, F32)

    ext_ref[8:8 + L, :] = xbc_ref[...]
    d_xbc = ext_ref.shape[1]
    bc = []
    for c0 in range(0, d_xbc, 512):
        acc = cb_ref[:, c0:c0 + 512] + cw_ref[0:1, c0:c0 + 512] * ext_ref[5:5 + L, c0:c0 + 512]
        for k in range(1, CONV_W):
            acc = acc + cw_ref[k:k + 1, c0:c0 + 512] * ext_ref[5 + k:5 + k + L, c0:c0 + 512]
        acc = _silu(acc)
        if c0 < d_inner:
            xs_ref[:, c0:c0 + 512] = acc
        else:
            bc.append(acc)
    ext_ref[0:8, :] = ext_ref[L:L + 8, :]
    b_all = jnp.concatenate(bc[:len(bc) // 2], axis=1)
    c_all = jnp.concatenate(bc[len(bc) // 2:], axis=1)

    x_raw = dt_ref[...] + dtb_ref[...]
    dtv = jnp.maximum(x_raw, 0.0) + jnp.log1p(jnp.exp(-jnp.abs(x_raw)))
    da = dtv * (-jnp.exp(alog_ref[...]))
    ri = lax.broadcasted_iota(jnp.int32, (L, L), 0)
    ci = lax.broadcasted_iota(jnp.int32, (L, L), 1)
    causal = ci <= ri
    acs = jnp.dot(causal.astype(F32), da, preferred_element_type=F32, precision=lax.Precision.HIGHEST)
    acs_t = acs.T
    acs_last = acs[L - 1:L, :]
    e_acs = jnp.exp(acs)
    tail = jnp.exp(acs_last - acs)
    e_last = jnp.exp(acs_last)
    lane_lo = lax.broadcasted_iota(jnp.int32, (1, LANES), 1) < SSM_HEADDIM

    cb = None
    for p in range(n_pairs):
        g = p // pairs_per_group
        if p % pairs_per_group == 0:
            bg = b_all[:, g * N:(g + 1) * N]
            cg = c_all[:, g * N:(g + 1) * N].astype(BF16)
            cb = lax.dot_general(cg, bg.astype(BF16), (((1,), (1,)), ((), ())), preferred_element_type=F32)
            bg_t = bg.T.astype(BF16)
        x2 = xs_ref[:, p * LANES:(p + 1) * LANES]
        xdt = x2 * _pair(dtv, p, lane_lo)
        y2 = None
        for half in range(2):
            h = 2 * p + half
            seg = acs[:, h:h + 1] - acs_t[h:h + 1, :]
            m = (cb * jnp.exp(jnp.where(causal, seg, -jnp.inf))).astype(BF16)
            keep = lane_lo if half == 0 else jnp.logical_not(lane_lo)
            part = jnp.dot(m, jnp.where(keep, xdt, 0.0).astype(BF16), preferred_element_type=F32)
            y2 = part if y2 is None else y2 + part
        st = state_ref[:, p * LANES:(p + 1) * LANES]
        y2 = y2 + jnp.dot(cg, st.astype(BF16), preferred_element_type=F32) * _pair(e_acs, p, lane_lo)
        y2 = y2 + dsk_ref[:, p * LANES:(p + 1) * LANES] * x2
        y_ref[:, p * LANES:(p + 1) * LANES] = y2
        w2 = (xdt * _pair(tail, p, lane_lo)).astype(BF16)
        state_ref[:, p * LANES:(p + 1) * LANES] = (
            st * _pair(e_last, p, lane_lo) + jnp.dot(bg_t, w2, preferred_element_type=F32))

    for g in range(n_groups):
        sl = slice(g * gw, (g + 1) * gw)
        yg = y_ref[:, sl] * _silu(z_ref[:, sl])
        o_ref[:, sl] = _rms(yg, ng_ref[:, sl]).astype(o_ref.dtype)


def _ssd(xbc, z, dt_raw, conv_w, conv_b, dt_bias, a_log, d_skip, norm_g):
    bsz, s_len, d_xbc = xbc.shape
    d_inner = z.shape[-1]
    n_heads = d_inner // SSM_HEADDIM
    L = SSD_CHUNK
    pad = LANES - n_heads
    row = lambda v: jnp.pad(v.astype(F32), (0, pad)).reshape(1, LANES)
    const = lambda b, c: (0, 0)
    blk = lambda w: pl.BlockSpec((None, L, w), lambda b, c: (b, c, 0))
    return pl.pallas_call(
        functools.partial(_ssd_kernel, d_inner=d_inner, n_groups=SSM_GROUPS),
        grid=(bsz, s_len // L),
        in_specs=[blk(d_xbc), blk(d_inner), blk(LANES),
                  pl.BlockSpec((CONV_W, d_xbc), const),
                  pl.BlockSpec((1, d_xbc), const),
                  pl.BlockSpec((1, LANES), const),
                  pl.BlockSpec((1, LANES), const),
                  pl.BlockSpec((1, d_inner), const),
                  pl.BlockSpec((1, d_inner), const)],
        out_specs=blk(d_inner),
        out_shape=jax.ShapeDtypeStruct((bsz, s_len, d_inner), BF16),
        scratch_shapes=[pltpu.VMEM((L + 8, d_xbc), F32),
                        pltpu.VMEM((L, d_inner), F32),
                        pltpu.VMEM((SSM_STATE, d_inner), F32),
                        pltpu.VMEM((L, d_inner), F32)],
        compiler_params=_params("parallel", "arbitrary"),
        name="ssd",
    )(xbc, z, dt_raw, conv_w, conv_b.reshape(1, d_xbc), row(dt_bias), row(a_log),
      jnp.repeat(d_skip.astype(F32), SSM_HEADDIM).reshape(1, d_inner), norm_g.reshape(1, d_inner))


def _rope_tables(s_len):
    half = ROT_DIMS // 2
    inv_freq = ROPE_THETA ** (-jnp.arange(half, dtype=F32) / half)
    ang = jnp.arange(s_len).astype(F32)[:, None] * inv_freq[None, :]
    cos, sin = jnp.cos(ang), jnp.sin(ang)
    rest = NSA_HEAD_DIM - ROT_DIMS
    one = jnp.ones((s_len, rest), F32)
    zero = jnp.zeros((s_len, rest), F32)
    zh = jnp.zeros((s_len, half), F32)
    t_cos = jnp.concatenate([cos, cos, one], axis=1)
    t_up = jnp.concatenate([-sin, zh, zero], axis=1)
    t_dn = jnp.concatenate([zh, sin, zero], axis=1)
    two = lambda t: jnp.concatenate([t, t], axis=1)
    return two(t_cos), two(t_up), two(t_dn)


def _rope(x, t_cos, t_up, t_dn):
    half = ROT_DIMS // 2
    return x * t_cos + pltpu.roll(x, LANES - half, axis=1) * t_up + pltpu.roll(x, half, axis=1) * t_dn


def _q_post_kernel(q_ref, tc_ref, tu_ref, td_ref, qc_ref, qr_ref):
    scale = NSA_HEAD_DIM ** -0.5
    lane_lo = lax.broadcasted_iota(jnp.int32, (1, LANES), 1) < NSA_HEAD_DIM
    for col in range(q_ref.shape[1] // LANES):
        x = q_ref[:, col * LANES:(col + 1) * LANES] * scale
        xr = _rope(x, tc_ref[...], tu_ref[...], td_ref[...])
        for half in range(2):
            head = 2 * col + half
            kvh, g = head // NSA_GROUP, head % NSA_GROUP
            for src, dst in ((x, qc_ref), (xr, qr_ref)):
                v = src if half == 0 else pltpu.roll(src, NSA_HEAD_DIM, axis=1)
                dst[kvh, g * Q_BLOCK:(g + 1) * Q_BLOCK, :] = jnp.where(lane_lo, v, 0.0).astype(BF16)


def _q_post(q, tables):
    bsz, s_len, dq = q.shape
    nq = s_len // Q_BLOCK
    tab = pl.BlockSpec((Q_BLOCK, LANES), lambda b, i: (i, 0))
    out = pl.BlockSpec((None, None, NSA_KV_HEADS, NSA_GROUP * Q_BLOCK, LANES), lambda b, i: (b, i, 0, 0, 0))
    shape = jax.ShapeDtypeStruct((bsz, nq, NSA_KV_HEADS, NSA_GROUP * Q_BLOCK, LANES), BF16)
    return pl.pallas_call(
        _q_post_kernel,
        grid=(bsz, nq),
        in_specs=[pl.BlockSpec((None, Q_BLOCK, dq), lambda b, i: (b, i, 0)), tab, tab, tab],
        out_specs=[out, out],
        out_shape=[shape, shape],
        compiler_params=_params("parallel", "parallel"),
        name="q_post",
    )(q, *tables)


def _kv_post_kernel(ks_ref, vs_ref, kw_ref, vw_ref, tc_ref, tu_ref, td_ref,
                    ksa_ref, vsa_ref, kwt_ref, vwa_ref):
    tm = ks_ref.shape[0]
    chunk = pl.program_id(1)
    lane = lax.broadcasted_iota(jnp.int32, (1, LANES), 1)
    lane_lo = lane < NSA_HEAD_DIM
    ones_col = (lane == NSA_HEAD_DIM).astype(F32)
    row_lo = lax.broadcasted_iota(jnp.int32, (LANES, 1), 0) < NSA_HEAD_DIM
    blk_row = lax.broadcasted_iota(jnp.int32, (LANES, tm), 0)
    key_blk = chunk * (tm // SEL_BLOCK) + lax.broadcasted_iota(jnp.int32, (LANES, tm), 1) // SEL_BLOCK
    expand = (blk_row == key_blk).astype(BF16)
    for col in range(ks_ref.shape[1] // LANES):
        sl = slice(col * LANES, (col + 1) * LANES)
        ks_t = _rope(ks_ref[:, sl], tc_ref[...], tu_ref[...], td_ref[...]).T
        kw_t = _rope(kw_ref[:, sl], tc_ref[...], tu_ref[...], td_ref[...]).T
        vs, vw = vs_ref[:, sl], vw_ref[:, sl]
        for half in range(2):
            h = 2 * col + half
            if half == 1:
                ks_t = pltpu.roll(ks_t, NSA_HEAD_DIM, axis=0)
                kw_t = pltpu.roll(kw_t, NSA_HEAD_DIM, axis=0)
                vs = pltpu.roll(vs, NSA_HEAD_DIM, axis=1)
                vw = pltpu.roll(vw, NSA_HEAD_DIM, axis=1)
            ksa_ref[h, 0:LANES, :] = expand
            ksa_ref[h, LANES:2 * LANES, :] = jnp.where(row_lo, ks_t, 0.0).astype(BF16)
            kw_pad = jnp.where(row_lo, kw_t, 0.0).astype(BF16)
            for u in range(tm // Q_BLOCK):
                kwt_ref[h, u] = kw_pad[:, u * Q_BLOCK:(u + 1) * Q_BLOCK]
            vsa_ref[h] = jnp.where(lane_lo, vs, ones_col).astype(BF16)
            vwa_ref[h] = jnp.where(lane_lo, vw, ones_col).astype(BF16)


def _kv_post(ks, vs, kw, vw, tables):
    bsz, s_len, dk = ks.shape
    tm = KEY_CHUNK
    n_ch = s_len // tm
    upc = tm // Q_BLOCK
    inp = pl.BlockSpec((None, tm, dk), lambda b, c: (b, c, 0))
    tab = pl.BlockSpec((tm, LANES), lambda b, c: (c, 0))
    hk = NSA_KV_HEADS
    outs = pl.pallas_call(
        _kv_post_kernel,
        grid=(bsz, n_ch),
        in_specs=[inp, inp, inp, inp, tab, tab, tab],
        out_specs=[pl.BlockSpec((None, hk, None, 2 * LANES, tm), lambda b, c: (b, 0, c, 0, 0)),
                   pl.BlockSpec((None, hk, None, tm, LANES), lambda b, c: (b, 0, c, 0, 0)),
                   pl.BlockSpec((None, hk, upc, LANES, Q_BLOCK), lambda b, c: (b, 0, c, 0, 0)),
                   pl.BlockSpec((None, hk, None, tm, LANES), lambda b, c: (b, 0, c, 0, 0))],
        out_shape=[jax.ShapeDtypeStruct((bsz, hk, n_ch, 2 * LANES, tm), BF16),
                   jax.ShapeDtypeStruct((bsz, hk, n_ch, tm, LANES), BF16),
                   jax.ShapeDtypeStruct((bsz, hk, n_ch * upc, LANES, Q_BLOCK), BF16),
                   jax.ShapeDtypeStruct((bsz, hk, n_ch, tm, LANES), BF16)],
        compiler_params=_params("parallel", "parallel"),
        name="kv_post",
    )(ks, vs, kw, vw, *tables)
    ks_aug, vs_aug, kw_t, vw_aug = outs
    return ks_aug, vs_aug, kw_t, vw_aug.reshape(bsz, hk, n_ch * upc, Q_BLOCK, LANES)


def _compress_kernel(x_ref, pe_ref, w1_ref, w2_ref, o_ref, *, transpose_out):
    half = w1_ref.shape[0] // 2
    xb = x_ref[...].astype(BF16)
    y0 = jnp.dot(xb, w1_ref[0:half, :], preferred_element_type=F32)
    y1 = jnp.dot(xb, w1_ref[half:2 * half, :], preferred_element_type=F32)
    bias = jnp.dot(pe_ref[...].astype(BF16), w1_ref[...], preferred_element_type=F32)[0:1, :]
    n = y0.shape[0]
    hid = y0 + pltpu.roll(y1, n - 1, axis=0) + bias
    out = jnp.dot(_silu(hid).astype(BF16), w2_ref[...], preferred_element_type=F32)
    if transpose_out:
        o_ref[...] = out.T.astype(BF16)
    else:
        lane = lax.broadcasted_iota(jnp.int32, (1, LANES), 1)
        o_ref[...] = jnp.where(lane < NSA_HEAD_DIM, out, (lane == NSA_HEAD_DIM).astype(F32)).astype(BF16)


def _compress(t, pe, w1, w2, transpose_out):
    bsz, s_len, _ = t.shape
    hk, dk = NSA_KV_HEADS, NSA_HEAD_DIM
    n = s_len // CMP_STRIDE
    xr = t.reshape(bsz, n, CMP_STRIDE, hk, dk).transpose(0, 3, 1, 2, 4).reshape(bsz, hk, n, CMP_STRIDE * dk)
    pe8 = jnp.broadcast_to(pe.reshape(1, CMP_BLOCK * dk), (8, CMP_BLOCK * dk))
    w2p = jnp.pad(w2, ((0, 0), (0, LANES - dk))).astype(BF16)
    o_shape = (bsz, hk, LANES, n) if transpose_out else (bsz, hk, n, LANES)
    const = lambda b, h: (0, 0)
    return pl.pallas_call(
        functools.partial(_compress_kernel, transpose_out=transpose_out),
        grid=(bsz, hk),
        in_specs=[pl.BlockSpec((None, None, n, CMP_STRIDE * dk), lambda b, h: (b, h, 0, 0)),
                  pl.BlockSpec(pe8.shape, const),
                  pl.BlockSpec(w1.shape, const),
                  pl.BlockSpec(w2p.shape, const)],
        out_specs=pl.BlockSpec((None, None) + o_shape[2:], lambda b, h: (b, h, 0, 0)),
        out_shape=jax.ShapeDtypeStruct(o_shape, BF16),
        compiler_params=_params("parallel", "parallel"),
        name="compress",
    )(xr, pe8, w1.astype(BF16), w2p)


def _importance_matrix(n_cmp_rows, n_blocks):
    c = np.arange(n_cmp_rows)[:, None]
    j = np.arange(n_blocks)[None, :]
    per = SEL_BLOCK // CMP_STRIDE
    a = ((c >= per * j) & (c <= per * j + per - 1)).astype(np.float32)
    for r in range(1, CMP_BLOCK // CMP_STRIDE):
        a += ((c + r >= per * j) & (c + r <= per * j + per - 1)).astype(np.float32)
    return a


def _nsa_kernel(qc_ref, qr_ref, gt_ref, kct_ref, vca_ref, ksa_ref, vsa_ref, kwt_ref, vwa_ref, imp_ref,
                o_ref, sc_ref):
    i = pl.program_id(2)
    rows = NSA_GROUP * Q_BLOCK
    n_cmp = kct_ref.shape[1]
    qc = qc_ref[...]
    qr = qr_ref[...]
    tok = lax.broadcasted_iota(jnp.int32, (rows, 1), 0) % Q_BLOCK
    t = i * Q_BLOCK + tok

    s = jnp.dot(qc, kct_ref[...], preferred_element_type=F32)
    cmp_end = lax.broadcasted_iota(jnp.int32, (1, n_cmp), 1) * CMP_STRIDE + (CMP_BLOCK - 1)
    s = jnp.where(cmp_end <= t, s, -jnp.inf)
    m = jnp.max(s, axis=-1, keepdims=True)
    m = jnp.where(m == -jnp.inf, 0.0, m)
    p = jnp.exp(s - m)
    d = jnp.sum(p, axis=-1, keepdims=True)
    p = p / jnp.where(d > 0, d, 1.0)
    o_cmp = jnp.dot(p.astype(BF16), vca_ref[...], preferred_element_type=F32)

    p_grp = p[0:Q_BLOCK]
    for g in range(1, NSA_GROUP):
        p_grp = p_grp + p[g * Q_BLOCK:(g + 1) * Q_BLOCK]
    imp = jnp.dot(p_grp, imp_ref[...], preferred_element_type=F32, precision=lax.Precision.HIGHEST)
    nb = imp.shape[1]
    tl = lax.broadcasted_iota(jnp.int32, (Q_BLOCK, nb), 0)
    jb = lax.broadcasted_iota(jnp.int32, (Q_BLOCK, nb), 1)
    qblk = (i * Q_BLOCK + tl) // SEL_BLOCK
    forced = (jb == 0) | (jb == qblk) | (jb == qblk - 1)
    score = jnp.where(forced, jnp.inf, jnp.where(jb <= qblk, imp, -1.0))
    sc_t = score.T
    sc_ref[...] = sc_t
    blk_idx = lax.broadcasted_iota(jnp.int32, sc_t.shape, 0)

    def rank_body(jp, cnt):
        r = sc_ref[pl.ds(jp, 1), :]
        ahead = (r > sc_t) | ((r == sc_t) & (blk_idx > jp))
        return cnt + ahead.astype(F32)

    n_causal = (i + 1) * (Q_BLOCK // SEL_BLOCK)
    rank = lax.fori_loop(0, jnp.minimum(n_causal, nb), rank_body, jnp.zeros(sc_t.shape, F32))
    bias = jnp.where(rank < float(N_SEL), 0.0, -MASK_BIAS).T.astype(BF16)
    q_aug = jnp.concatenate([jnp.concatenate([bias] * NSA_GROUP, axis=0), qr], axis=1)

    key_in = lax.broadcasted_iota(jnp.int32, (1, KEY_CHUNK), 1)

    def sel_chunk(ck, carry, diagonal):
        m_run, acc = carry
        sc = jnp.dot(q_aug, ksa_ref[ck], preferred_element_type=F32)
        if diagonal:
            sc = jnp.where(ck * KEY_CHUNK + key_in <= t, sc, -jnp.inf)
        m_new = jnp.maximum(m_run, jnp.max(sc, axis=-1, keepdims=True))
        pp = jnp.exp(sc - m_new)
        acc = jnp.exp(m_run - m_new) * acc + jnp.dot(pp.astype(BF16), vsa_ref[ck], preferred_element_type=F32)
        return m_new, acc

    last = (i * Q_BLOCK) // KEY_CHUNK
    carry = (jnp.full((rows, 1), -jnp.inf, F32), jnp.zeros((rows, LANES), F32))
    carry = lax.fori_loop(0, last, lambda ck, c: sel_chunk(ck, c, False), carry)
    _, acc_s = sel_chunk(last, carry, True)
    o_sel = acc_s / acc_s[:, NSA_HEAD_DIM:NSA_HEAD_DIM + 1]

    units = [jnp.maximum(i - (WIN_UNITS - 1) + u, 0) for u in range(WIN_UNITS)]
    kw = jnp.concatenate([kwt_ref[u] for u in units], axis=1)
    vw = jnp.concatenate([vwa_ref[u] for u in units], axis=0)
    sw = jnp.dot(qr, kw, preferred_element_type=F32)
    cw = lax.broadcasted_iota(jnp.int32, (1, WIN_UNITS * Q_BLOCK), 1)
    wmask = (cw <= tok + WINDOW) & (cw > tok) & (cw // Q_BLOCK + i >= WIN_UNITS - 1)
    sw = jnp.where(wmask, sw, -jnp.inf)
    pw = jnp.exp(sw - jnp.max(sw, axis=-1, keepdims=True))
    acc_w = jnp.dot(pw.astype(BF16), vw, preferred_element_type=F32)
    o_win = acc_w / acc_w[:, NSA_HEAD_DIM:NSA_HEAD_DIM + 1]

    gates = jax.nn.sigmoid(gt_ref[...])
    lane_lo = lax.broadcasted_iota(jnp.int32, (1, LANES), 1) < NSA_HEAD_DIM
    heads = []
    for g in range(NSA_GROUP):
        rs = slice(g * Q_BLOCK, (g + 1) * Q_BLOCK)
        c0 = g * N_BRANCH
        heads.append(gates[:, c0:c0 + 1] * o_cmp[rs] + gates[:, c0 + 1:c0 + 2] * o_sel[rs]
                     + gates[:, c0 + 2:c0 + 3] * o_win[rs])
    pairs = [jnp.where(lane_lo, heads[2 * k], pltpu.roll(heads[2 * k + 1], NSA_HEAD_DIM, axis=1))
             for k in range(NSA_GROUP // 2)]
    o_ref[...] = jnp.concatenate(pairs, axis=1).astype(o_ref.dtype)


def _nsa_attention(qc, qr, gates_raw, kc_t, vc_aug, ks_aug, vs_aug, kw_t, vw_aug, imp_mat):
    bsz, nq, hk, rows, _ = qc.shape
    s_len = nq * Q_BLOCK
    n_ch = ks_aug.shape[2]
    n_units = kw_t.shape[2]
    n_cmp = kc_t.shape[3]
    nb = imp_mat.shape[1]
    qspec = pl.BlockSpec((None, None, None, rows, LANES), lambda b, h, i: (b, i, h, 0, 0))
    per_head = lambda shp: pl.BlockSpec((None, None) + shp, lambda b, h, i: (b, h) + (0,) * len(shp))
    return pl.pallas_call(
        _nsa_kernel,
        grid=(bsz, hk, nq),
        in_specs=[qspec, qspec,
                  pl.BlockSpec((None, Q_BLOCK, LANES), lambda b, h, i: (b, i, h)),
                  per_head((LANES, n_cmp)),
                  per_head((n_cmp, LANES)),
                  per_head((n_ch, 2 * LANES, KEY_CHUNK)),
                  per_head((n_ch, KEY_CHUNK, LANES)),
                  per_head((n_units, LANES, Q_BLOCK)),
                  per_head((n_units, Q_BLOCK, LANES)),
                  pl.BlockSpec(imp_mat.shape, lambda b, h, i: (0, 0))],
        out_specs=pl.BlockSpec((None, Q_BLOCK, NSA_GROUP * NSA_HEAD_DIM), lambda b, h, i: (b, i, h)),
        out_shape=jax.ShapeDtypeStruct((bsz, s_len, hk * NSA_GROUP * NSA_HEAD_DIM), BF16),
        scratch_shapes=[pltpu.VMEM((nb, Q_BLOCK), F32)],
        compiler_params=_params("parallel", "parallel", "arbitrary"),
        name="nsa_attention",
    )(qc, qr, gates_raw, kc_t, vc_aug, ks_aug, vs_aug, kw_t, vw_aug, imp_mat)


def kernel(x, c, mod_w, mod_b, norm_pre_mix, norm_post_mix, norm_pre_ffn, norm_post_ffn, ffn_w_in, ffn_w_out, ssm_w_in, ssm_conv_w, ssm_conv_b, ssm_dt_bias, ssm_a_log, ssm_d, ssm_norm, ssm_w_out, kv_norm, kv_mod_w, kv_mod_b, w_kv, cmp_pos_k, cmp_w1_k, cmp_w2_k, cmp_pos_v, cmp_w1_v, cmp_w2_v, nsa_w_q, nsa_w_o):
    bsz, s_len, d = x.shape
    depth = mod_w.shape[0]
    n_a = ssm_w_in.shape[0]
    d_ff = ffn_w_out.shape[1]
    d_inner = ssm_w_out.shape[1]
    n_ssm_heads = d_inner // SSM_HEADDIM
    d_xbc = ssm_conv_w.shape[-1]
    dq = NSA_HEADS * NSA_HEAD_DIM
    dkv = NSA_KV_HEADS * NSA_HEAD_DIM
    assert s_len % KEY_CHUNK == 0 and s_len // SEL_BLOCK <= LANES and s_len // SEL_BLOCK >= N_SEL

    mods = _mods(c, mod_w, mod_b)
    kv_mods = _mods(c, kv_mod_w[None], kv_mod_b[None])[0]
    vec = lambda m, k: m[:, k * d:(k + 1) * d].reshape(bsz, 1, d)

    shared = None
    tables = None
    for i in range(depth):
        sh_m, sc_m, g_m, sh_f, sc_f, g_f = [vec(mods[i], k) for k in range(6)]
        if i == n_a:
            tables = _rope_tables(s_len)
            w_kv_b = w_kv.astype(BF16)
            kc, vc, ks, vs, kw, vw = _adaln_mm(x, kv_norm, vec(kv_mods, 0), vec(kv_mods, 1),
                                               [w_kv_b[:, k * dkv:(k + 1) * dkv] for k in range(6)])
            kc_t = _compress(kc, cmp_pos_k, cmp_w1_k, cmp_w2_k, transpose_out=True)
            vc_aug = _compress(vc, cmp_pos_v, cmp_w1_v, cmp_w2_v, transpose_out=False)
            shared = (kc_t, vc_aug) + _kv_post(ks, vs, kw, vw, tables)
            imp_mat = jnp.asarray(_importance_matrix(s_len // CMP_STRIDE, LANES))
        if i < n_a:
            w_in = ssm_w_in[i].astype(BF16)
            w_dt = jnp.pad(w_in[:, d_inner + d_xbc:], ((0, 0), (0, LANES - n_ssm_heads)))
            z, xbc, dt_raw = _adaln_mm(x, norm_pre_mix[i], sh_m, sc_m,
                                       [w_in[:, :d_inner], w_in[:, d_inner:d_inner + d_xbc], w_dt])
            y = _ssd(xbc, z, dt_raw, ssm_conv_w[i], ssm_conv_b[i], ssm_dt_bias[i], ssm_a_log[i],
                     ssm_d[i], ssm_norm[i])
            x = _mm_post(y, ssm_w_out[i].astype(BF16), x, g_m, norm_post_mix[i])
        else:
            w_q = nsa_w_q[i - n_a].astype(BF16)
            w_g = w_q[:, dq:].reshape(d, NSA_KV_HEADS, NSA_GROUP * N_BRANCH)
            w_g = jnp.pad(w_g, ((0, 0), (0, 0), (0, LANES - NSA_GROUP * N_BRANCH))).reshape(d, NSA_KV_HEADS * LANES)
            q, gates_raw = _adaln_mm(x, norm_pre_mix[i], sh_m, sc_m, [w_q[:, :dq], w_g])
            qc, qr = _q_post(q, tables)
            o = _nsa_attention(qc, qr, gates_raw, *shared, imp_mat)
            x = _mm_post(o, nsa_w_o[i - n_a].astype(BF16), x, g_m, norm_post_mix[i])
        w_ffn = ffn_w_in[i].astype(BF16)
        x = _ffn(x, norm_pre_ffn[i], sh_f, sc_f, w_ffn[:, :d_ff], w_ffn[:, d_ff:], ffn_w_out[i].astype(BF16),
                 norm_post_ffn[i], g_f)
    return x
```

```python
import functools
import math

import numpy as np
import jax
import jax.numpy as jnp
from jax import lax
from jax.experimental import pallas as pl
from jax.experimental.pallas import tpu as pltpu

F32 = jnp.float32
BF16 = jnp.bfloat16

EPS = 1e-6
DEPTH = 4
N_A = DEPTH // 2

SSM_HEADDIM = 64
SSM_STATE = 128
SSM_GROUPS = 4
CONV_W = 4
SSD_CHUNK = 256

NSA_HEADS = 16
NSA_KV_HEADS = 4
NSA_GROUP = NSA_HEADS // NSA_KV_HEADS
NSA_HEAD_DIM = 64
CMP_BLOCK = 32
CMP_STRIDE = 16
SEL_BLOCK = 64
N_SEL = 16
WINDOW = 512
Q_BLOCK = 128
N_BRANCH = 3
ROPE_THETA = 500000.0
ROT_DIMS = NSA_HEAD_DIM // 4

LANES = 128
KEY_CHUNK = 512
WIN_UNITS = WINDOW // Q_BLOCK + 1
MASK_BIAS = float(2 ** 30)
VMEM_LIMIT = 56 * 1024 * 1024


def _params(*sem):
    return pltpu.CompilerParams(dimension_semantics=sem, vmem_limit_bytes=VMEM_LIMIT)


def _rms(x, g):
    return x * lax.rsqrt(jnp.mean(x * x, axis=-1, keepdims=True) + EPS) * g


def _silu(x):
    return x * jax.nn.sigmoid(x)


def _mods_kernel(c_ref, w_ref, b_ref, o_ref):
    cs = _silu(c_ref[...])
    o_ref[...] = jnp.dot(cs, w_ref[...], preferred_element_type=F32) + b_ref[...]


def _mods(c, w, b):
    n_l, d, n = w.shape
    bsz = c.shape[0]
    tn = 1024
    return pl.pallas_call(
        _mods_kernel,
        grid=(n_l, n // tn),
        in_specs=[pl.BlockSpec((bsz, d), lambda l, j: (0, 0)),
                  pl.BlockSpec((None, d, tn), lambda l, j: (l, 0, j)),
                  pl.BlockSpec((None, 1, tn), lambda l, j: (l, 0, j))],
        out_specs=pl.BlockSpec((None, bsz, tn), lambda l, j: (l, 0, j)),
        out_shape=jax.ShapeDtypeStruct((n_l, bsz, n), F32),
        compiler_params=_params("parallel", "parallel"),
        name="mods",
    )(c, w, b.reshape(n_l, 1, n))


def _adaln_mm_kernel(x_ref, g_ref, sh_ref, sc_ref, *refs, n_w):
    h = _rms(x_ref[...], g_ref[...]) * (1.0 + sc_ref[...]) + sh_ref[...]
    hb = h.astype(BF16)
    for w_ref, o_ref in zip(refs[:n_w], refs[n_w:]):
        o_ref[...] = jnp.dot(hb, w_ref[...], preferred_element_type=F32).astype(o_ref.dtype)


def _adaln_mm(x, g, shift, scale, ws, tm=256):
    bsz, s_len, d = x.shape
    n_w = len(ws)
    const = lambda b, i: (0, 0)
    return pl.pallas_call(
        functools.partial(_adaln_mm_kernel, n_w=n_w),
        grid=(bsz, s_len // tm),
        in_specs=[pl.BlockSpec((None, tm, d), lambda b, i: (b, i, 0)),
                  pl.BlockSpec((1, d), const),
                  pl.BlockSpec((None, 1, d), lambda b, i: (b, 0, 0)),
                  pl.BlockSpec((None, 1, d), lambda b, i: (b, 0, 0))]
                 + [pl.BlockSpec(w.shape, const) for w in ws],
        out_specs=[pl.BlockSpec((None, tm, w.shape[1]), lambda b, i: (b, i, 0)) for w in ws],
        out_shape=[jax.ShapeDtypeStruct((bsz, s_len, w.shape[1]), F32) for w in ws],
        compiler_params=_params("parallel", "parallel"),
        name="adaln_mm",
    )(x, g.reshape(1, d), shift, scale, *ws)


def _mm_post_kernel(a_ref, w_ref, x_ref, gate_ref, g_ref, o_ref):
    y = jnp.dot(a_ref[...], w_ref[...], preferred_element_type=F32)
    o_ref[...] = x_ref[...] + gate_ref[...] * _rms(y, g_ref[...])


def _mm_post(a, w, x, gate, g, tm=512):
    bsz, s_len, d = x.shape
    k = a.shape[-1]
    return pl.pallas_call(
        _mm_post_kernel,
        grid=(bsz, s_len // tm),
        in_specs=[pl.BlockSpec((None, tm, k), lambda b, i: (b, i, 0)),
                  pl.BlockSpec((k, d), lambda b, i: (0, 0)),
                  pl.BlockSpec((None, tm, d), lambda b, i: (b, i, 0)),
                  pl.BlockSpec((None, 1, d), lambda b, i: (b, 0, 0)),
                  pl.BlockSpec((1, d), lambda b, i: (0, 0))],
        out_specs=pl.BlockSpec((None, tm, d), lambda b, i: (b, i, 0)),
        out_shape=jax.ShapeDtypeStruct(x.shape, F32),
        compiler_params=_params("parallel", "parallel"),
        name="mm_post",
    )(a, w, x, gate, g.reshape(1, d))


def _ffn_kernel(x_ref, g1_ref, sh_ref, sc_ref, wa_ref, wb_ref, wo_ref, g2_ref, gate_ref, o_ref, u_ref, *, tc):
    x = x_ref[...]
    hb = (_rms(x, g1_ref[...]) * (1.0 + sc_ref[...]) + sh_ref[...]).astype(BF16)
    d_ff = wa_ref.shape[1]
    for c0 in range(0, d_ff, tc):
        a = jnp.dot(hb, wa_ref[:, c0:c0 + tc], preferred_element_type=F32)
        b = jnp.dot(hb, wb_ref[:, c0:c0 + tc], preferred_element_type=F32)
        u_ref[:, c0:c0 + tc] = (_silu(a) * b).astype(BF16)
    y = jnp.dot(u_ref[...], wo_ref[...], preferred_element_type=F32)
    o_ref[...] = x + gate_ref[...] * _rms(y, g2_ref[...])


def _ffn(x, g1, shift, scale, wa, wb, wo, g2, gate, tm=512, tc=256):
    bsz, s_len, d = x.shape
    d_ff = wa.shape[1]
    const = lambda b, i: (0, 0)
    vec = lambda b, i: (b, 0, 0)
    return pl.pallas_call(
        functools.partial(_ffn_kernel, tc=tc),
        grid=(bsz, s_len // tm),
        in_specs=[pl.BlockSpec((None, tm, d), lambda b, i: (b, i, 0)),
                  pl.BlockSpec((1, d), const),
                  pl.BlockSpec((None, 1, d), vec),
                  pl.BlockSpec((None, 1, d), vec),
                  pl.BlockSpec((d, d_ff), const),
                  pl.BlockSpec((d, d_ff), const),
                  pl.BlockSpec((d_ff, d), const),
                  pl.BlockSpec((1, d), const),
                  pl.BlockSpec((None, 1, d), vec)],
        out_specs=pl.BlockSpec((None, tm, d), lambda b, i: (b, i, 0)),
        out_shape=jax.ShapeDtypeStruct(x.shape, F32),
        scratch_shapes=[pltpu.VMEM((tm, d_ff), BF16)],
        compiler_params=_params("parallel", "parallel"),
        name="ffn",
    )(x, g1.reshape(1, d), shift, scale, wa, wb, wo, g2.reshape(1, d), gate)


def _pair(mat, p, lane_lo):
    return jnp.where(lane_lo, mat[:, 2 * p:2 * p + 1], mat[:, 2 * p + 1:2 * p + 2])


def _ssd_kernel(xbc_ref, z_ref, dt_ref, cw_ref, cb_ref, dtb_ref, alog_ref, dsk_ref, ng_ref, o_ref,
                ext_ref, xs_ref, state_ref, y_ref, *, d_inner, n_groups):
    L = SSD_CHUNK
    N = SSM_STATE
    gw = d_inner // n_groups
    pairs_per_group = gw // LANES
    n_pairs = d_inner // LANES
    c_idx = pl.program_id(1)

    @pl.when(c_idx == 0)
    def _():
        ext_ref[0:8, :] = jnp.zeros((8, ext_ref.shape[1]), F32)
        state_ref[...] = jnp.zeros(state_ref.shape, F32)

    ext_ref[8:8 + L, :] = xbc_ref[...]
    d_xbc = ext_ref.shape[1]
    bc = []
    for c0 in range(0, d_xbc, 512):
        acc = cb_ref[:, c0:c0 + 512] + cw_ref[0:1, c0:c0 + 512] * ext_ref[5:5 + L, c0:c0 + 512]
        for k in range(1, CONV_W):
            acc = acc + cw_ref[k:k + 1, c0:c0 + 512] * ext_ref[5 + k:5 + k + L, c0:c0 + 512]
        acc = _silu(acc)
        if c0 < d_inner:
            xs_ref[:, c0:c0 + 512] = acc
        else:
            bc.append(acc)
    ext_ref[0:8, :] = ext_ref[L:L + 8, :]
    b_all = jnp.concatenate(bc[:len(bc) // 2], axis=1)
    c_all = jnp.concatenate(bc[len(bc) // 2:], axis=1)

    x_raw = dt_ref[...] + dtb_ref[...]
    dtv = jnp.maximum(x_raw, 0.0) + jnp.log1p(jnp.exp(-jnp.abs(x_raw)))
    da = dtv * (-jnp.exp(alog_ref[...]))
    ri = lax.broadcasted_iota(jnp.int32, (L, L), 0)
    ci = lax.broadcasted_iota(jnp.int32, (L, L), 1)
    causal = ci <= ri
    acs = jnp.dot(causal.astype(F32), da, preferred_element_type=F32, precision=lax.Precision.HIGHEST)
    acs_t = acs.T
    acs_last = acs[L - 1:L, :]
    e_acs = jnp.exp(acs)
    tail = jnp.exp(acs_last - acs)
    e_last = jnp.exp(acs_last)
    lane_lo = lax.broadcasted_iota(jnp.int32, (1, LANES), 1) < SSM_HEADDIM

    cb = None
    for p in range(n_pairs):
        g = p // pairs_per_group
        if p % pairs_per_group == 0:
            bg = b_all[:, g * N:(g + 1) * N]
            cg = c_all[:, g * N:(g + 1) * N].astype(BF16)
            cb = lax.dot_general(cg, bg.astype(BF16), (((1,), (1,)), ((), ())), preferred_element_type=F32)
            bg_t = bg.T.astype(BF16)
        x2 = xs_ref[:, p * LANES:(p + 1) * LANES]
        xdt = x2 * _pair(dtv, p, lane_lo)
        y2 = None
        for half in range(2):
            h = 2 * p + half
            seg = acs[:, h:h + 1] - acs_t[h:h + 1, :]
            m = (cb * jnp.exp(jnp.where(causal, seg, -jnp.inf))).astype(BF16)
            keep = lane_lo if half == 0 else jnp.logical_not(lane_lo)
            part = jnp.dot(m, jnp.where(keep, xdt, 0.0).astype(BF16), preferred_element_type=F32)
            y2 = part if y2 is None else y2 + part
        st = state_ref[:, p * LANES:(p + 1) * LANES]
        y2 = y2 + jnp.dot(cg, st.astype(BF16), preferred_element_type=F32) * _pair(e_acs, p, lane_lo)
        y2 = y2 + dsk_ref[:, p * LANES:(p + 1) * LANES] * x2
        y_ref[:, p * LANES:(p + 1) * LANES] = y2
        w2 = (xdt * _pair(tail, p, lane_lo)).astype(BF16)
        state_ref[:, p * LANES:(p + 1) * LANES] = (
            st * _pair(e_last, p, lane_lo) + jnp.dot(bg_t, w2, preferred_element_type=F32))

    for g in range(n_groups):
        sl = slice(g * gw, (g + 1) * gw)
        yg = y_ref[:, sl] * _silu(z_ref[:, sl])
        o_ref[:, sl] = _rms(yg, ng_ref[:, sl]).astype(o_ref.dtype)


def _ssd(xbc, z, dt_raw, conv_w, conv_b, dt_bias, a_log, d_skip, norm_g):
    bsz, s_len, d_xbc = xbc.shape
    d_inner = z.shape[-1]
    n_heads = d_inner // SSM_HEADDIM
    L = SSD_CHUNK
    pad = LANES - n_heads
    row = lambda v: jnp.pad(v.astype(F32), (0, pad)).reshape(1, LANES)
    const = lambda b, c: (0, 0)
    blk = lambda w: pl.BlockSpec((None, L, w), lambda b, c: (b, c, 0))
    return pl.pallas_call(
        functools.partial(_ssd_kernel, d_inner=d_inner, n_groups=SSM_GROUPS),
        grid=(bsz, s_len // L),
        in_specs=[blk(d_xbc), blk(d_inner), blk(LANES),
                  pl.BlockSpec((CONV_W, d_xbc), const),
                  pl.BlockSpec((1, d_xbc), const),
                  pl.BlockSpec((1, LANES), const),
                  pl.BlockSpec((1, LANES), const),
                  pl.BlockSpec((1, d_inner), const),
                  pl.BlockSpec((1, d_inner), const)],
        out_specs=blk(d_inner),
        out_shape=jax.ShapeDtypeStruct((bsz, s_len, d_inner), BF16),
        scratch_shapes=[pltpu.VMEM((L + 8, d_xbc), F32),
                        pltpu.VMEM((L, d_inner), F32),
                        pltpu.VMEM((SSM_STATE, d_inner), F32),
                        pltpu.VMEM((L, d_inner), F32)],
        compiler_params=_params("parallel", "arbitrary"),
        name="ssd",
    )(xbc, z, dt_raw, conv_w, conv_b.reshape(1, d_xbc), row(dt_bias), row(a_log),
      jnp.repeat(d_skip.astype(F32), SSM_HEADDIM).reshape(1, d_inner), norm_g.reshape(1, d_inner))


def _rope_tables(s_len):
    half = ROT_DIMS // 2
    inv_freq = ROPE_THETA ** (-jnp.arange(half, dtype=F32) / half)
    ang = jnp.arange(s_len).astype(F32)[:, None] * inv_freq[None, :]
    cos, sin = jnp.cos(ang), jnp.sin(ang)
    rest = NSA_HEAD_DIM - ROT_DIMS
    one = jnp.ones((s_len, rest), F32)
    zero = jnp.zeros((s_len, rest), F32)
    zh = jnp.zeros((s_len, half), F32)
    t_cos = jnp.concatenate([cos, cos, one], axis=1)
    t_up = jnp.concatenate([-sin, zh, zero], axis=1)
    t_dn = jnp.concatenate([zh, sin, zero], axis=1)
    two = lambda t: jnp.concatenate([t, t], axis=1)
    return two(t_cos), two(t_up), two(t_dn)


def _rope(x, t_cos, t_up, t_dn):
    half = ROT_DIMS // 2
    return x * t_cos + pltpu.roll(x, LANES - half, axis=1) * t_up + pltpu.roll(x, half, axis=1) * t_dn


def _q_post_kernel(q_ref, tc_ref, tu_ref, td_ref, qc_ref, qr_ref):
    scale = NSA_HEAD_DIM ** -0.5
    lane_lo = lax.broadcasted_iota(jnp.int32, (1, LANES), 1) < NSA_HEAD_DIM
    for col in range(q_ref.shape[1] // LANES):
        x = q_ref[:, col * LANES:(col + 1) * LANES] * scale
        xr = _rope(x, tc_ref[...], tu_ref[...], td_ref[...])
        for half in range(2):
            head = 2 * col + half
            kvh, g = head // NSA_GROUP, head % NSA_GROUP
            for src, dst in ((x, qc_ref), (xr, qr_ref)):
                v = src if half == 0 else pltpu.roll(src, NSA_HEAD_DIM, axis=1)
                dst[kvh, g * Q_BLOCK:(g + 1) * Q_BLOCK, :] = jnp.where(lane_lo, v, 0.0).astype(BF16)


def _q_post(q, tables):
    bsz, s_len, dq = q.shape
    nq = s_len // Q_BLOCK
    tab = pl.BlockSpec((Q_BLOCK, LANES), lambda b, i: (i, 0))
    out = pl.BlockSpec((None, None, NSA_KV_HEADS, NSA_GROUP * Q_BLOCK, LANES), lambda b, i: (b, i, 0, 0, 0))
    shape = jax.ShapeDtypeStruct((bsz, nq, NSA_KV_HEADS, NSA_GROUP * Q_BLOCK, LANES), BF16)
    return pl.pallas_call(
        _q_post_kernel,
        grid=(bsz, nq),
        in_specs=[pl.BlockSpec((None, Q_BLOCK, dq), lambda b, i: (b, i, 0)), tab, tab, tab],
        out_specs=[out, out],
        out_shape=[shape, shape],
        compiler_params=_params("parallel", "parallel"),
        name="q_post",
    )(q, *tables)


def _kv_post_kernel(ks_ref, vs_ref, kw_ref, vw_ref, tc_ref, tu_ref, td_ref,
                    ksa_ref, vsa_ref, kwt_ref, vwa_ref):
    tm = ks_ref.shape[0]
    chunk = pl.program_id(1)
    lane = lax.broadcasted_iota(jnp.int32, (1, LANES), 1)
    lane_lo = lane < NSA_HEAD_DIM
    ones_col = (lane == NSA_HEAD_DIM).astype(F32)
    row_lo = lax.broadcasted_iota(jnp.int32, (LANES, 1), 0) < NSA_HEAD_DIM
    blk_row = lax.broadcasted_iota(jnp.int32, (LANES, tm), 0)
    key_blk = chunk * (tm // SEL_BLOCK) + lax.broadcasted_iota(jnp.int32, (LANES, tm), 1) // SEL_BLOCK
    expand = (blk_row == key_blk).astype(BF16)
    for col in range(ks_ref.shape[1] // LANES):
        sl = slice(col * LANES, (col + 1) * LANES)
        ks_t = _rope(ks_ref[:, sl], tc_ref[...], tu_ref[...], td_ref[...]).T
        kw_t = _rope(kw_ref[:, sl], tc_ref[...], tu_ref[...], td_ref[...]).T
        vs, vw = vs_ref[:, sl], vw_ref[:, sl]
        for half in range(2):
            h = 2 * col + half
            if half == 1:
                ks_t = pltpu.roll(ks_t, NSA_HEAD_DIM, axis=0)
                kw_t = pltpu.roll(kw_t, NSA_HEAD_DIM, axis=0)
                vs = pltpu.roll(vs, NSA_HEAD_DIM, axis=1)
                vw = pltpu.roll(vw, NSA_HEAD_DIM, axis=1)
            ksa_ref[h, 0:LANES, :] = expand
            ksa_ref[h, LANES:2 * LANES, :] = jnp.where(row_lo, ks_t, 0.0).astype(BF16)
            kw_pad = jnp.where(row_lo, kw_t, 0.0).astype(BF16)
            for u in range(tm // Q_BLOCK):
                kwt_ref[h, u] = kw_pad[:, u * Q_BLOCK:(u + 1) * Q_BLOCK]
            vsa_ref[h] = jnp.where(lane_lo, vs, ones_col).astype(BF16)
            vwa_ref[h] = jnp.where(lane_lo, vw, ones_col).astype(BF16)


def _kv_post(ks, vs, kw, vw, tables):
    bsz, s_len, dk = ks.shape
    tm = KEY_CHUNK
    n_ch = s_len // tm
    upc = tm // Q_BLOCK
    inp = pl.BlockSpec((None, tm, dk), lambda b, c: (b, c, 0))
    tab = pl.BlockSpec((tm, LANES), lambda b, c: (c, 0))
    hk = NSA_KV_HEADS
    outs = pl.pallas_call(
        _kv_post_kernel,
        grid=(bsz, n_ch),
        in_specs=[inp, inp, inp, inp, tab, tab, tab],
        out_specs=[pl.BlockSpec((None, hk, None, 2 * LANES, tm), lambda b, c: (b, 0, c, 0, 0)),
                   pl.BlockSpec((None, hk, None, tm, LANES), lambda b, c: (b, 0, c, 0, 0)),
                   pl.BlockSpec((None, hk, upc, LANES, Q_BLOCK), lambda b, c: (b, 0, c, 0, 0)),
                   pl.BlockSpec((None, hk, None, tm, LANES), lambda b, c: (b, 0, c, 0, 0))],
        out_shape=[jax.ShapeDtypeStruct((bsz, hk, n_ch, 2 * LANES, tm), BF16),
                   jax.ShapeDtypeStruct((bsz, hk, n_ch, tm, LANES), BF16),
                   jax.ShapeDtypeStruct((bsz, hk, n_ch * upc, LANES, Q_BLOCK), BF16),
                   jax.ShapeDtypeStruct((bsz, hk, n_ch, tm, LANES), BF16)],
        compiler_params=_params("parallel", "parallel"),
        name="kv_post",
    )(ks, vs, kw, vw, *tables)
    ks_aug, vs_aug, kw_t, vw_aug = outs
    return ks_aug, vs_aug, kw_t, vw_aug.reshape(bsz, hk, n_ch * upc, Q_BLOCK, LANES)


def _compress_kernel(x_ref, pe_ref, w1_ref, w2_ref, o_ref, *, transpose_out):
    half = w1_ref.shape[0] // 2
    xb = x_ref[...].astype(BF16)
    y0 = jnp.dot(xb, w1_ref[0:half, :], preferred_element_type=F32)
    y1 = jnp.dot(xb, w1_ref[half:2 * half, :], preferred_element_type=F32)
    bias = jnp.dot(pe_ref[...].astype(BF16), w1_ref[...], preferred_element_type=F32)[0:1, :]
    n = y0.shape[0]
    hid = y0 + pltpu.roll(y1, n - 1, axis=0) + bias
    out = jnp.dot(_silu(hid).astype(BF16), w2_ref[...], preferred_element_type=F32)
    if transpose_out:
        o_ref[...] = out.T.astype(BF16)
    else:
        lane = lax.broadcasted_iota(jnp.int32, (1, LANES), 1)
        o_ref[...] = jnp.where(lane < NSA_HEAD_DIM, out, (lane == NSA_HEAD_DIM).astype(F32)).astype(BF16)


def _compress(t, pe, w1, w2, transpose_out):
    bsz, s_len, _ = t.shape
    hk, dk = NSA_KV_HEADS, NSA_HEAD_DIM
    n = s_len // CMP_STRIDE
    xr = t.reshape(bsz, n, CMP_STRIDE, hk, dk).transpose(0, 3, 1, 2, 4).reshape(bsz, hk, n, CMP_STRIDE * dk)
    pe8 = jnp.broadcast_to(pe.reshape(1, CMP_BLOCK * dk), (8, CMP_BLOCK * dk))
    w2p = jnp.pad(w2, ((0, 0), (0, LANES - dk))).astype(BF16)
    o_shape = (bsz, hk, LANES, n) if transpose_out else (bsz, hk, n, LANES)
    const = lambda b, h: (0, 0)
    return pl.pallas_call(
        functools.partial(_compress_kernel, transpose_out=transpose_out),
        grid=(bsz, hk),
        in_specs=[pl.BlockSpec((None, None, n, CMP_STRIDE * dk), lambda b, h: (b, h, 0, 0)),
                  pl.BlockSpec(pe8.shape, const),
                  pl.BlockSpec(w1.shape, const),
                  pl.BlockSpec(w2p.shape, const)],
        out_specs=pl.BlockSpec((None, None) + o_shape[2:], lambda b, h: (b, h, 0, 0)),
        out_shape=jax.ShapeDtypeStruct(o_shape, BF16),
        compiler_params=_params("parallel", "parallel"),
        name="compress",
    )(xr, pe8, w1.astype(BF16), w2p)


def _importance_matrix(n_cmp_rows, n_blocks):
    c = np.arange(n_cmp_rows)[:, None]
    j = np.arange(n_blocks)[None, :]
    per = SEL_BLOCK // CMP_STRIDE
    a = ((c >= per * j) & (c <= per * j + per - 1)).astype(np.float32)
    for r in range(1, CMP_BLOCK // CMP_STRIDE):
        a += ((c + r >= per * j) & (c + r <= per * j + per - 1)).astype(np.float32)
    return a


def _dot_split3(a, w):
    out = None
    for _ in range(3):
        part = a.astype(BF16)
        a = a - part.astype(F32)
        term = jnp.dot(part, w, preferred_element_type=F32)
        out = term if out is None else out + term
    return out


def _nsa_kernel(qc_ref, qr_ref, gt_ref, kct_ref, vca_ref, ksa_ref, vsa_ref, kwt_ref, vwa_ref, imp_ref,
                o_ref, sc_ref):
    i = pl.program_id(2)
    rows = NSA_GROUP * Q_BLOCK
    n_cmp = kct_ref.shape[1]
    qc = qc_ref[...]
    qr = qr_ref[...]
    tok = lax.broadcasted_iota(jnp.int32, (rows, 1), 0) % Q_BLOCK
    t = i * Q_BLOCK + tok

    s = jnp.dot(qc, kct_ref[...], preferred_element_type=F32)
    cmp_end = lax.broadcasted_iota(jnp.int32, (1, n_cmp), 1) * CMP_STRIDE + (CMP_BLOCK - 1)
    s = jnp.where(cmp_end <= t, s, -jnp.inf)
    m = jnp.max(s, axis=-1, keepdims=True)
    m = jnp.where(m == -jnp.inf, 0.0, m)
    p = jnp.exp(s - m)
    d = jnp.sum(p, axis=-1, keepdims=True)
    p = p * (1.0 / jnp.where(d > 0, d, 1.0))
    o_cmp = jnp.dot(p.astype(BF16), vca_ref[...], preferred_element_type=F32)

    units = [jnp.maximum(i - (WIN_UNITS - 1) + u, 0) for u in range(WIN_UNITS)]
    kw = jnp.concatenate([kwt_ref[u] for u in units], axis=1)
    vw = jnp.concatenate([vwa_ref[u] for u in units], axis=0)
    sw = jnp.dot(qr, kw, preferred_element_type=F32)
    cw = lax.broadcasted_iota(jnp.int32, (1, WIN_UNITS * Q_BLOCK), 1)
    wmask = (cw <= tok + WINDOW) & (cw > tok) & (cw // Q_BLOCK + i >= WIN_UNITS - 1)
    sw = jnp.where(wmask, sw, -jnp.inf)
    pw = jnp.exp(sw - jnp.max(sw, axis=-1, keepdims=True))
    acc_w = jnp.dot(pw.astype(BF16), vw, preferred_element_type=F32)
    o_win = acc_w * (1.0 / acc_w[:, NSA_HEAD_DIM:NSA_HEAD_DIM + 1])

    p_grp = p[0:Q_BLOCK]
    for g in range(1, NSA_GROUP):
        p_grp = p_grp + p[g * Q_BLOCK:(g + 1) * Q_BLOCK]
    imp = _dot_split3(p_grp, imp_ref[...])
    nb = imp.shape[1]
    tl = lax.broadcasted_iota(jnp.int32, (Q_BLOCK, nb), 0)
    jb = lax.broadcasted_iota(jnp.int32, (Q_BLOCK, nb), 1)
    qblk = (i * Q_BLOCK + tl) // SEL_BLOCK
    forced = (jb == 0) | (jb == qblk) | (jb == qblk - 1)
    score = jnp.where(forced, jnp.inf, jnp.where(jb <= qblk, imp, -1.0))
    sc_t = score.T

    def count_ge(v):
        return jnp.sum((sc_t >= v).astype(F32), axis=0, keepdims=True)

    top = jnp.max(sc_t, axis=0, keepdims=True)
    tau = top
    done = count_ge(top) >= N_SEL
    for _ in range(N_SEL - 1):
        top = jnp.max(jnp.where(sc_t < top, sc_t, -2.0), axis=0, keepdims=True)
        tau = jnp.where(done, tau, top)
        done = done | (count_ge(top) >= N_SEL)
    above = sc_t > tau
    tied = sc_t == tau
    room = N_SEL - jnp.sum(above.astype(F32), axis=0, keepdims=True)
    blk_r = lax.broadcasted_iota(jnp.int32, (nb, nb), 0)
    blk_c = lax.broadcasted_iota(jnp.int32, (nb, nb), 1)
    ties_before = jnp.dot((blk_c < blk_r).astype(BF16), tied.astype(BF16), preferred_element_type=F32)
    chosen = above | (tied & (ties_before < room))
    blk_t = lax.broadcasted_iota(jnp.int32, sc_t.shape, 0)
    qblk_t = (i * Q_BLOCK + lax.broadcasted_iota(jnp.int32, sc_t.shape, 1)) // SEL_BLOCK
    bias = jnp.where(chosen & (blk_t <= qblk_t), 0.0, -MASK_BIAS).T.astype(BF16)
    q_aug = jnp.concatenate([jnp.concatenate([bias] * NSA_GROUP, axis=0), qr], axis=1)

    def scores(ck, slot):
        sc = jnp.dot(q_aug, ksa_ref[ck], preferred_element_type=F32)
        sc_ref[slot] = sc
        return jnp.max(sc, axis=-1, keepdims=True)

    def absorb(sc, sc_max, ck, m_run, acc):
        m_new = jnp.maximum(m_run, sc_max)
        pp = jnp.exp(sc - m_new)
        acc = jnp.exp(m_run - m_new) * acc + jnp.dot(pp.astype(BF16), vsa_ref[ck], preferred_element_type=F32)
        return m_new, acc

    last = (i * Q_BLOCK) // KEY_CHUNK

    def pair_body(kk, carry):
        mx0, m_run, acc = carry
        mx1 = scores(2 * kk + 1, 1)
        m_run, acc = absorb(sc_ref[0], mx0, 2 * kk, m_run, acc)
        mx0 = scores(2 * kk + 2, 0)
        m_run, acc = absorb(sc_ref[1], mx1, 2 * kk + 1, m_run, acc)
        return mx0, m_run, acc

    def odd_body(_, carry):
        mx0, m_run, acc = carry
        mx1 = scores(last, 1)
        m_run, acc = absorb(sc_ref[0], mx0, last - 1, m_run, acc)
        return mx1, m_run, acc

    carry = (scores(0, 0), jnp.full((rows, 1), -jnp.inf, F32), jnp.zeros((rows, LANES), F32))
    carry = lax.fori_loop(0, last // 2, pair_body, carry)
    _, m_run, acc_s = lax.fori_loop(0, last % 2, odd_body, carry)
    key_pos = last * KEY_CHUNK + lax.broadcasted_iota(jnp.int32, (1, KEY_CHUNK), 1)
    sc = jnp.where(key_pos <= t, sc_ref[last % 2], -jnp.inf)
    _, acc_s = absorb(sc, jnp.max(sc, axis=-1, keepdims=True), last, m_run, acc_s)
    o_sel = acc_s * (1.0 / acc_s[:, NSA_HEAD_DIM:NSA_HEAD_DIM + 1])

    gates = jax.nn.sigmoid(gt_ref[...])
    lane_lo = lax.broadcasted_iota(jnp.int32, (1, LANES), 1) < NSA_HEAD_DIM
    heads = []
    for g in range(NSA_GROUP):
        rs = slice(g * Q_BLOCK, (g + 1) * Q_BLOCK)
        c0 = g * N_BRANCH
        heads.append(gates[:, c0:c0 + 1] * o_cmp[rs] + gates[:, c0 + 1:c0 + 2] * o_sel[rs]
                     + gates[:, c0 + 2:c0 + 3] * o_win[rs])
    pairs = [jnp.where(lane_lo, heads[2 * k], pltpu.roll(heads[2 * k + 1], NSA_HEAD_DIM, axis=1))
             for k in range(NSA_GROUP // 2)]
    o_ref[...] = jnp.concatenate(pairs, axis=1).astype(o_ref.dtype)


def _nsa_attention(qc, qr, gates_raw, kc_t, vc_aug, ks_aug, vs_aug, kw_t, vw_aug, imp_mat):
    bsz, nq, hk, rows, _ = qc.shape
    s_len = nq * Q_BLOCK
    n_ch = ks_aug.shape[2]
    n_units = kw_t.shape[2]
    n_cmp = kc_t.shape[3]
    nb = imp_mat.shape[1]
    qspec = pl.BlockSpec((None, None, None, rows, LANES), lambda b, h, i: (b, i, h, 0, 0))
    per_head = lambda shp: pl.BlockSpec((None, None) + shp, lambda b, h, i: (b, h) + (0,) * len(shp))
    return pl.pallas_call(
        _nsa_kernel,
        grid=(bsz, hk, nq),
        in_specs=[qspec, qspec,
                  pl.BlockSpec((None, Q_BLOCK, LANES), lambda b, h, i: (b, i, h)),
                  per_head((LANES, n_cmp)),
                  per_head((n_cmp, LANES)),
                  per_head((n_ch, 2 * LANES, KEY_CHUNK)),
                  per_head((n_ch, KEY_CHUNK, LANES)),
                  per_head((n_units, LANES, Q_BLOCK)),
                  per_head((n_units, Q_BLOCK, LANES)),
                  pl.BlockSpec(imp_mat.shape, lambda b, h, i: (0, 0))],
        out_specs=pl.BlockSpec((None, Q_BLOCK, NSA_GROUP * NSA_HEAD_DIM), lambda b, h, i: (b, i, h)),
        out_shape=jax.ShapeDtypeStruct((bsz, s_len, hk * NSA_GROUP * NSA_HEAD_DIM), BF16),
        scratch_shapes=[pltpu.VMEM((2, rows, KEY_CHUNK), F32)],
        compiler_params=_params("parallel", "parallel", "arbitrary"),
        name="nsa_attention",
    )(qc, qr, gates_raw, kc_t, vc_aug, ks_aug, vs_aug, kw_t, vw_aug, imp_mat)


def kernel(x, c, mod_w, mod_b, norm_pre_mix, norm_post_mix, norm_pre_ffn, norm_post_ffn, ffn_w_in, ffn_w_out, ssm_w_in, ssm_conv_w, ssm_conv_b, ssm_dt_bias, ssm_a_log, ssm_d, ssm_norm, ssm_w_out, kv_norm, kv_mod_w, kv_mod_b, w_kv, cmp_pos_k, cmp_w1_k, cmp_w2_k, cmp_pos_v, cmp_w1_v, cmp_w2_v, nsa_w_q, nsa_w_o):
    bsz, s_len, d = x.shape
    depth = mod_w.shape[0]
    n_a = ssm_w_in.shape[0]
    d_ff = ffn_w_out.shape[1]
    d_inner = ssm_w_out.shape[1]
    n_ssm_heads = d_inner // SSM_HEADDIM
    d_xbc = ssm_conv_w.shape[-1]
    dq = NSA_HEADS * NSA_HEAD_DIM
    dkv = NSA_KV_HEADS * NSA_HEAD_DIM
    assert s_len % KEY_CHUNK == 0 and s_len // SEL_BLOCK <= LANES and s_len // SEL_BLOCK >= N_SEL

    mods = _mods(c, mod_w, mod_b)
    kv_mods = _mods(c, kv_mod_w[None], kv_mod_b[None])[0]
    vec = lambda m, k: m[:, k * d:(k + 1) * d].reshape(bsz, 1, d)

    shared = None
    tables = None
    for i in range(depth):
        sh_m, sc_m, g_m, sh_f, sc_f, g_f = [vec(mods[i], k) for k in range(6)]
        if i == n_a:
            tables = _rope_tables(s_len)
            w_kv_b = w_kv.astype(BF16)
            kc, vc, ks, vs, kw, vw = _adaln_mm(x, kv_norm, vec(kv_mods, 0), vec(kv_mods, 1),
                                               [w_kv_b[:, k * dkv:(k + 1) * dkv] for k in range(6)])
            kc_t = _compress(kc, cmp_pos_k, cmp_w1_k, cmp_w2_k, transpose_out=True)
            vc_aug = _compress(vc, cmp_pos_v, cmp_w1_v, cmp_w2_v, transpose_out=False)
            shared = (kc_t, vc_aug) + _kv_post(ks, vs, kw, vw, tables)
            imp_mat = jnp.asarray(_importance_matrix(s_len // CMP_STRIDE, LANES), dtype=BF16)
        if i < n_a:
            w_in = ssm_w_in[i].astype(BF16)
            w_dt = jnp.pad(w_in[:, d_inner + d_xbc:], ((0, 0), (0, LANES - n_ssm_heads)))
            z, xbc, dt_raw = _adaln_mm(x, norm_pre_mix[i], sh_m, sc_m,
                                       [w_in[:, :d_inner], w_in[:, d_inner:d_inner + d_xbc], w_dt])
            y = _ssd(xbc, z, dt_raw, ssm_conv_w[i], ssm_conv_b[i], ssm_dt_bias[i], ssm_a_log[i],
                     ssm_d[i], ssm_norm[i])
            x = _mm_post(y, ssm_w_out[i].astype(BF16), x, g_m, norm_post_mix[i])
        else:
            w_q = nsa_w_q[i - n_a].astype(BF16)
            w_g = w_q[:, dq:].reshape(d, NSA_KV_HEADS, NSA_GROUP * N_BRANCH)
            w_g = jnp.pad(w_g, ((0, 0), (0, 0), (0, LANES - NSA_GROUP * N_BRANCH))).reshape(d, NSA_KV_HEADS * LANES)
            q, gates_raw = _adaln_mm(x, norm_pre_mix[i], sh_m, sc_m, [w_q[:, :dq], w_g])
            qc, qr = _q_post(q, tables)
            o = _nsa_attention(qc, qr, gates_raw, *shared, imp_mat)
            x = _mm_post(o, nsa_w_o[i - n_a].astype(BF16), x, g_m, norm_post_mix[i])
        w_ffn = ffn_w_in[i].astype(BF16)
        x = _ffn(x, norm_pre_ffn[i], sh_f, sc_f, w_ffn[:, :d_ff], w_ffn[:, d_ff:], ffn_w_out[i].astype(BF16),
                 norm_post_ffn[i], g_f)
    return x
```

```python
import functools
import math

import numpy as np
import jax
import jax.numpy as jnp
from jax import lax
from jax.experimental import pallas as pl
from jax.experimental.pallas import tpu as pltpu

F32 = jnp.float32
BF16 = jnp.bfloat16

EPS = 1e-6
DEPTH = 4
N_A = DEPTH // 2

SSM_HEADDIM = 64
SSM_STATE = 128
SSM_GROUPS = 4
CONV_W = 4
SSD_CHUNK = 256

NSA_HEADS = 16
NSA_KV_HEADS = 4
NSA_GROUP = NSA_HEADS // NSA_KV_HEADS
NSA_HEAD_DIM = 64
CMP_BLOCK = 32
CMP_STRIDE = 16
SEL_BLOCK = 64
N_SEL = 16
WINDOW = 512
Q_BLOCK = 128
N_BRANCH = 3
ROPE_THETA = 500000.0
ROT_DIMS = NSA_HEAD_DIM // 4

LANES = 128
KEY_CHUNK = 512
WIN_UNITS = WINDOW // Q_BLOCK + 1
MASK_BIAS = float(2 ** 30)
VMEM_LIMIT = 56 * 1024 * 1024


def _params(*sem):
    return pltpu.CompilerParams(dimension_semantics=sem, vmem_limit_bytes=VMEM_LIMIT)


def _rms(x, g):
    return x * lax.rsqrt(jnp.mean(x * x, axis=-1, keepdims=True) + EPS) * g


def _silu(x):
    return x * jax.nn.sigmoid(x)


def _mods_kernel(c_ref, w_ref, b_ref, o_ref):
    cs = _silu(c_ref[...])
    o_ref[...] = jnp.dot(cs, w_ref[...], preferred_element_type=F32) + b_ref[...]


def _mods(c, w, b):
    n_l, d, n = w.shape
    bsz = c.shape[0]
    tn = 1024
    return pl.pallas_call(
        _mods_kernel,
        grid=(n_l, n // tn),
        in_specs=[pl.BlockSpec((bsz, d), lambda l, j: (0, 0)),
                  pl.BlockSpec((None, d, tn), lambda l, j: (l, 0, j)),
                  pl.BlockSpec((None, 1, tn), lambda l, j: (l, 0, j))],
        out_specs=pl.BlockSpec((None, bsz, tn), lambda l, j: (l, 0, j)),
        out_shape=jax.ShapeDtypeStruct((n_l, bsz, n), F32),
        compiler_params=_params("parallel", "parallel"),
        name="mods",
    )(c, w, b.reshape(n_l, 1, n))


def _adaln_mm_kernel(x_ref, g_ref, sh_ref, sc_ref, *refs, n_w):
    h = _rms(x_ref[...], g_ref[...]) * (1.0 + sc_ref[...]) + sh_ref[...]
    hb = h.astype(BF16)
    for w_ref, o_ref in zip(refs[:n_w], refs[n_w:]):
        o_ref[...] = jnp.dot(hb, w_ref[...], preferred_element_type=F32).astype(o_ref.dtype)


def _adaln_mm(x, g, shift, scale, ws, tm=256):
    bsz, s_len, d = x.shape
    n_w = len(ws)
    const = lambda b, i: (0, 0)
    return pl.pallas_call(
        functools.partial(_adaln_mm_kernel, n_w=n_w),
        grid=(bsz, s_len // tm),
        in_specs=[pl.BlockSpec((None, tm, d), lambda b, i: (b, i, 0)),
                  pl.BlockSpec((1, d), const),
                  pl.BlockSpec((None, 1, d), lambda b, i: (b, 0, 0)),
                  pl.BlockSpec((None, 1, d), lambda b, i: (b, 0, 0))]
                 + [pl.BlockSpec(w.shape, const) for w in ws],
        out_specs=[pl.BlockSpec((None, tm, w.shape[1]), lambda b, i: (b, i, 0)) for w in ws],
        out_shape=[jax.ShapeDtypeStruct((bsz, s_len, w.shape[1]), F32) for w in ws],
        compiler_params=_params("parallel", "parallel"),
        name="adaln_mm",
    )(x, g.reshape(1, d), shift, scale, *ws)


def _mm_post_kernel(a_ref, w_ref, x_ref, gate_ref, g_ref, o_ref):
    y = jnp.dot(a_ref[...], w_ref[...], preferred_element_type=F32)
    o_ref[...] = x_ref[...] + gate_ref[...] * _rms(y, g_ref[...])


def _mm_post(a, w, x, gate, g, tm=512):
    bsz, s_len, d = x.shape
    k = a.shape[-1]
    return pl.pallas_call(
        _mm_post_kernel,
        grid=(bsz, s_len // tm),
        in_specs=[pl.BlockSpec((None, tm, k), lambda b, i: (b, i, 0)),
                  pl.BlockSpec((k, d), lambda b, i: (0, 0)),
                  pl.BlockSpec((None, tm, d), lambda b, i: (b, i, 0)),
                  pl.BlockSpec((None, 1, d), lambda b, i: (b, 0, 0)),
                  pl.BlockSpec((1, d), lambda b, i: (0, 0))],
        out_specs=pl.BlockSpec((None, tm, d), lambda b, i: (b, i, 0)),
        out_shape=jax.ShapeDtypeStruct(x.shape, F32),
        compiler_params=_params("parallel", "parallel"),
        name="mm_post",
    )(a, w, x, gate, g.reshape(1, d))


def _ffn_kernel(x_ref, g1_ref, sh_ref, sc_ref, wa_ref, wb_ref, wo_ref, g2_ref, gate_ref, o_ref, u_ref, *, tc):
    x = x_ref[...]
    hb = (_rms(x, g1_ref[...]) * (1.0 + sc_ref[...]) + sh_ref[...]).astype(BF16)
    d_ff = wa_ref.shape[1]
    for c0 in range(0, d_ff, tc):
        a = jnp.dot(hb, wa_ref[:, c0:c0 + tc], preferred_element_type=F32)
        b = jnp.dot(hb, wb_ref[:, c0:c0 + tc], preferred_element_type=F32)
        u_ref[:, c0:c0 + tc] = (_silu(a) * b).astype(BF16)
    y = jnp.dot(u_ref[...], wo_ref[...], preferred_element_type=F32)
    o_ref[...] = x + gate_ref[...] * _rms(y, g2_ref[...])


def _ffn(x, g1, shift, scale, wa, wb, wo, g2, gate, tm=512, tc=256):
    bsz, s_len, d = x.shape
    d_ff = wa.shape[1]
    const = lambda b, i: (0, 0)
    vec = lambda b, i: (b, 0, 0)
    return pl.pallas_call(
        functools.partial(_ffn_kernel, tc=tc),
        grid=(bsz, s_len // tm),
        in_specs=[pl.BlockSpec((None, tm, d), lambda b, i: (b, i, 0)),
                  pl.BlockSpec((1, d), const),
                  pl.BlockSpec((None, 1, d), vec),
                  pl.BlockSpec((None, 1, d), vec),
                  pl.BlockSpec((d, d_ff), const),
                  pl.BlockSpec((d, d_ff), const),
                  pl.BlockSpec((d_ff, d), const),
                  pl.BlockSpec((1, d), const),
                  pl.BlockSpec((None, 1, d), vec)],
        out_specs=pl.BlockSpec((None, tm, d), lambda b, i: (b, i, 0)),
        out_shape=jax.ShapeDtypeStruct(x.shape, F32),
        scratch_shapes=[pltpu.VMEM((tm, d_ff), BF16)],
        compiler_params=_params("parallel", "parallel"),
        name="ffn",
    )(x, g1.reshape(1, d), shift, scale, wa, wb, wo, g2.reshape(1, d), gate)


def _pair(mat, p, lane_lo):
    return jnp.where(lane_lo, mat[:, 2 * p:2 * p + 1], mat[:, 2 * p + 1:2 * p + 2])


def _ssd_kernel(xbc_ref, z_ref, dt_ref, cw_ref, cb_ref, dtb_ref, alog_ref, dsk_ref, ng_ref, o_ref,
                ext_ref, xs_ref, state_ref, y_ref, *, d_inner, n_groups):
    L = SSD_CHUNK
    N = SSM_STATE
    gw = d_inner // n_groups
    pairs_per_group = gw // LANES
    n_pairs = d_inner // LANES
    c_idx = pl.program_id(1)

    @pl.when(c_idx == 0)
    def _():
        ext_ref[0:8, :] = jnp.zeros((8, ext_ref.shape[1]), F32)
        state_ref[...] = jnp.zeros(state_ref.shape, F32)

    ext_ref[8:8 + L, :] = xbc_ref[...]
    d_xbc = ext_ref.shape[1]
    bc = []
    for c0 in range(0, d_xbc, 512):
        acc = cb_ref[:, c0:c0 + 512] + cw_ref[0:1, c0:c0 + 512] * ext_ref[5:5 + L, c0:c0 + 512]
        for k in range(1, CONV_W):
            acc = acc + cw_ref[k:k + 1, c0:c0 + 512] * ext_ref[5 + k:5 + k + L, c0:c0 + 512]
        acc = _silu(acc)
        if c0 < d_inner:
            xs_ref[:, c0:c0 + 512] = acc
        else:
            bc.append(acc)
    ext_ref[0:8, :] = ext_ref[L:L + 8, :]
    b_all = jnp.concatenate(bc[:len(bc) // 2], axis=1)
    c_all = jnp.concatenate(bc[len(bc) // 2:], axis=1)

    x_raw = dt_ref[...] + dtb_ref[...]
    dtv = jnp.maximum(x_raw, 0.0) + jnp.log1p(jnp.exp(-jnp.abs(x_raw)))
    da = dtv * (-jnp.exp(alog_ref[...]))
    ri = lax.broadcasted_iota(jnp.int32, (L, L), 0)
    ci = lax.broadcasted_iota(jnp.int32, (L, L), 1)
    causal = ci <= ri
    acs = jnp.dot(causal.astype(F32), da, preferred_element_type=F32, precision=lax.Precision.HIGHEST)
    acs_t = acs.T
    acs_last = acs[L - 1:L, :]
    e_acs = jnp.exp(acs)
    tail = jnp.exp(acs_last - acs)
    e_last = jnp.exp(acs_last)
    lane_lo = lax.broadcasted_iota(jnp.int32, (1, LANES), 1) < SSM_HEADDIM

    cb = None
    for p in range(n_pairs):
        g = p // pairs_per_group
        if p % pairs_per_group == 0:
            bg = b_all[:, g * N:(g + 1) * N]
            cg = c_all[:, g * N:(g + 1) * N].astype(BF16)
            cb = lax.dot_general(cg, bg.astype(BF16), (((1,), (1,)), ((), ())), preferred_element_type=F32)
            bg_t = bg.T.astype(BF16)
        x2 = xs_ref[:, p * LANES:(p + 1) * LANES]
        xdt = x2 * _pair(dtv, p, lane_lo)
        y2 = None
        for half in range(2):
            h = 2 * p + half
            seg = acs[:, h:h + 1] - acs_t[h:h + 1, :]
            m = (cb * jnp.exp(jnp.where(causal, seg, -jnp.inf))).astype(BF16)
            keep = lane_lo if half == 0 else jnp.logical_not(lane_lo)
            part = jnp.dot(m, jnp.where(keep, xdt, 0.0).astype(BF16), preferred_element_type=F32)
            y2 = part if y2 is None else y2 + part
        st = state_ref[:, p * LANES:(p + 1) * LANES]
        y2 = y2 + jnp.dot(cg, st.astype(BF16), preferred_element_type=F32) * _pair(e_acs, p, lane_lo)
        y2 = y2 + dsk_ref[:, p * LANES:(p + 1) * LANES] * x2
        y_ref[:, p * LANES:(p + 1) * LANES] = y2
        w2 = (xdt * _pair(tail, p, lane_lo)).astype(BF16)
        state_ref[:, p * LANES:(p + 1) * LANES] = (
            st * _pair(e_last, p, lane_lo) + jnp.dot(bg_t, w2, preferred_element_type=F32))

    for g in range(n_groups):
        sl = slice(g * gw, (g + 1) * gw)
        yg = y_ref[:, sl] * _silu(z_ref[:, sl])
        o_ref[:, sl] = _rms(yg, ng_ref[:, sl]).astype(o_ref.dtype)


def _ssd(xbc, z, dt_raw, conv_w, conv_b, dt_bias, a_log, d_skip, norm_g):
    bsz, s_len, d_xbc = xbc.shape
    d_inner = z.shape[-1]
    n_heads = d_inner // SSM_HEADDIM
    L = SSD_CHUNK
    pad = LANES - n_heads
    row = lambda v: jnp.pad(v.astype(F32), (0, pad)).reshape(1, LANES)
    const = lambda b, c: (0, 0)
    blk = lambda w: pl.BlockSpec((None, L, w), lambda b, c: (b, c, 0))
    return pl.pallas_call(
        functools.partial(_ssd_kernel, d_inner=d_inner, n_groups=SSM_GROUPS),
        grid=(bsz, s_len // L),
        in_specs=[blk(d_xbc), blk(d_inner), blk(LANES),
                  pl.BlockSpec((CONV_W, d_xbc), const),
                  pl.BlockSpec((1, d_xbc), const),
                  pl.BlockSpec((1, LANES), const),
                  pl.BlockSpec((1, LANES), const),
                  pl.BlockSpec((1, d_inner), const),
                  pl.BlockSpec((1, d_inner), const)],
        out_specs=blk(d_inner),
        out_shape=jax.ShapeDtypeStruct((bsz, s_len, d_inner), BF16),
        scratch_shapes=[pltpu.VMEM((L + 8, d_xbc), F32),
                        pltpu.VMEM((L, d_inner), F32),
                        pltpu.VMEM((SSM_STATE, d_inner), F32),
                        pltpu.VMEM((L, d_inner), F32)],
        compiler_params=_params("parallel", "arbitrary"),
        name="ssd",
    )(xbc, z, dt_raw, conv_w, conv_b.reshape(1, d_xbc), row(dt_bias), row(a_log),
      jnp.repeat(d_skip.astype(F32), SSM_HEADDIM).reshape(1, d_inner), norm_g.reshape(1, d_inner))


def _rope_tables(s_len):
    half = ROT_DIMS // 2
    inv_freq = ROPE_THETA ** (-jnp.arange(half, dtype=F32) / half)
    ang = jnp.arange(s_len).astype(F32)[:, None] * inv_freq[None, :]
    cos, sin = jnp.cos(ang), jnp.sin(ang)
    rest = NSA_HEAD_DIM - ROT_DIMS
    one = jnp.ones((s_len, rest), F32)
    zero = jnp.zeros((s_len, rest), F32)
    zh = jnp.zeros((s_len, half), F32)
    t_cos = jnp.concatenate([cos, cos, one], axis=1)
    t_up = jnp.concatenate([-sin, zh, zero], axis=1)
    t_dn = jnp.concatenate([zh, sin, zero], axis=1)
    two = lambda t: jnp.concatenate([t, t], axis=1)
    return two(t_cos), two(t_up), two(t_dn)


def _rope(x, t_cos, t_up, t_dn):
    half = ROT_DIMS // 2
    return x * t_cos + pltpu.roll(x, LANES - half, axis=1) * t_up + pltpu.roll(x, half, axis=1) * t_dn


def _q_post_kernel(q_ref, tc_ref, tu_ref, td_ref, qc_ref, qr_ref):
    scale = NSA_HEAD_DIM ** -0.5 * math.log2(math.e)
    lane_lo = lax.broadcasted_iota(jnp.int32, (1, LANES), 1) < NSA_HEAD_DIM
    for col in range(q_ref.shape[1] // LANES):
        x = q_ref[:, col * LANES:(col + 1) * LANES] * scale
        xr = _rope(x, tc_ref[...], tu_ref[...], td_ref[...])
        for half in range(2):
            head = 2 * col + half
            kvh, g = head // NSA_GROUP, head % NSA_GROUP
            for src, dst in ((x, qc_ref), (xr, qr_ref)):
                v = src if half == 0 else pltpu.roll(src, NSA_HEAD_DIM, axis=1)
                dst[kvh, g * Q_BLOCK:(g + 1) * Q_BLOCK, :] = jnp.where(lane_lo, v, 0.0).astype(BF16)


def _q_post(q, tables):
    bsz, s_len, dq = q.shape
    nq = s_len // Q_BLOCK
    tab = pl.BlockSpec((Q_BLOCK, LANES), lambda b, i: (i, 0))
    out = pl.BlockSpec((None, None, NSA_KV_HEADS, NSA_GROUP * Q_BLOCK, LANES), lambda b, i: (b, i, 0, 0, 0))
    shape = jax.ShapeDtypeStruct((bsz, nq, NSA_KV_HEADS, NSA_GROUP * Q_BLOCK, LANES), BF16)
    return pl.pallas_call(
        _q_post_kernel,
        grid=(bsz, nq),
        in_specs=[pl.BlockSpec((None, Q_BLOCK, dq), lambda b, i: (b, i, 0)), tab, tab, tab],
        out_specs=[out, out],
        out_shape=[shape, shape],
        compiler_params=_params("parallel", "parallel"),
        name="q_post",
    )(q, *tables)


def _kv_post_kernel(ks_ref, vs_ref, kw_ref, vw_ref, tc_ref, tu_ref, td_ref,
                    ksa_ref, vsa_ref, kwt_ref, vwa_ref):
    tm = ks_ref.shape[0]
    chunk = pl.program_id(1)
    lane = lax.broadcasted_iota(jnp.int32, (1, LANES), 1)
    lane_lo = lane < NSA_HEAD_DIM
    ones_col = (lane == NSA_HEAD_DIM).astype(F32)
    row_lo = lax.broadcasted_iota(jnp.int32, (LANES, 1), 0) < NSA_HEAD_DIM
    blk_row = lax.broadcasted_iota(jnp.int32, (LANES, tm), 0)
    key_blk = chunk * (tm // SEL_BLOCK) + lax.broadcasted_iota(jnp.int32, (LANES, tm), 1) // SEL_BLOCK
    expand = (blk_row == key_blk).astype(BF16)
    for col in range(ks_ref.shape[1] // LANES):
        sl = slice(col * LANES, (col + 1) * LANES)
        ks_t = _rope(ks_ref[:, sl], tc_ref[...], tu_ref[...], td_ref[...]).T
        kw_t = _rope(kw_ref[:, sl], tc_ref[...], tu_ref[...], td_ref[...]).T
        vs, vw = vs_ref[:, sl], vw_ref[:, sl]
        for half in range(2):
            h = 2 * col + half
            if half == 1:
                ks_t = pltpu.roll(ks_t, NSA_HEAD_DIM, axis=0)
                kw_t = pltpu.roll(kw_t, NSA_HEAD_DIM, axis=0)
                vs = pltpu.roll(vs, NSA_HEAD_DIM, axis=1)
                vw = pltpu.roll(vw, NSA_HEAD_DIM, axis=1)
            ksa_ref[h, 0:LANES, :] = expand
            ksa_ref[h, LANES:2 * LANES, :] = jnp.where(row_lo, ks_t, 0.0).astype(BF16)
            kw_pad = jnp.where(row_lo, kw_t, 0.0).astype(BF16)
            for u in range(tm // Q_BLOCK):
                kwt_ref[h, u] = kw_pad[:, u * Q_BLOCK:(u + 1) * Q_BLOCK]
            vsa_ref[h] = jnp.where(lane_lo, vs, ones_col).astype(BF16)
            vwa_ref[h] = jnp.where(lane_lo, vw, ones_col).astype(BF16)


def _kv_post(ks, vs, kw, vw, tables):
    bsz, s_len, dk = ks.shape
    tm = KEY_CHUNK
    n_ch = s_len // tm
    upc = tm // Q_BLOCK
    inp = pl.BlockSpec((None, tm, dk), lambda b, c: (b, c, 0))
    tab = pl.BlockSpec((tm, LANES), lambda b, c: (c, 0))
    hk = NSA_KV_HEADS
    outs = pl.pallas_call(
        _kv_post_kernel,
        grid=(bsz, n_ch),
        in_specs=[inp, inp, inp, inp, tab, tab, tab],
        out_specs=[pl.BlockSpec((None, hk, None, 2 * LANES, tm), lambda b, c: (b, 0, c, 0, 0)),
                   pl.BlockSpec((None, hk, None, tm, LANES), lambda b, c: (b, 0, c, 0, 0)),
                   pl.BlockSpec((None, hk, upc, LANES, Q_BLOCK), lambda b, c: (b, 0, c, 0, 0)),
                   pl.BlockSpec((None, hk, None, tm, LANES), lambda b, c: (b, 0, c, 0, 0))],
        out_shape=[jax.ShapeDtypeStruct((bsz, hk, n_ch, 2 * LANES, tm), BF16),
                   jax.ShapeDtypeStruct((bsz, hk, n_ch, tm, LANES), BF16),
                   jax.ShapeDtypeStruct((bsz, hk, n_ch * upc, LANES, Q_BLOCK), BF16),
                   jax.ShapeDtypeStruct((bsz, hk, n_ch, tm, LANES), BF16)],
        compiler_params=_params("parallel", "parallel"),
        name="kv_post",
    )(ks, vs, kw, vw, *tables)
    ks_aug, vs_aug, kw_t, vw_aug = outs
    return ks_aug, vs_aug, kw_t, vw_aug.reshape(bsz, hk, n_ch * upc, Q_BLOCK, LANES)


def _compress_kernel(x_ref, pe_ref, w1_ref, w2_ref, o_ref, *, transpose_out):
    half = w1_ref.shape[0] // 2
    xb = x_ref[...].astype(BF16)
    y0 = jnp.dot(xb, w1_ref[0:half, :], preferred_element_type=F32)
    y1 = jnp.dot(xb, w1_ref[half:2 * half, :], preferred_element_type=F32)
    bias = jnp.dot(pe_ref[...].astype(BF16), w1_ref[...], preferred_element_type=F32)[0:1, :]
    n = y0.shape[0]
    hid = y0 + pltpu.roll(y1, n - 1, axis=0) + bias
    out = jnp.dot(_silu(hid).astype(BF16), w2_ref[...], preferred_element_type=F32)
    if transpose_out:
        o_ref[...] = out.T.astype(BF16)
    else:
        lane = lax.broadcasted_iota(jnp.int32, (1, LANES), 1)
        o_ref[...] = jnp.where(lane < NSA_HEAD_DIM, out, (lane == NSA_HEAD_DIM).astype(F32)).astype(BF16)


def _compress(t, pe, w1, w2, transpose_out):
    bsz, s_len, _ = t.shape
    hk, dk = NSA_KV_HEADS, NSA_HEAD_DIM
    n = s_len // CMP_STRIDE
    xr = t.reshape(bsz, n, CMP_STRIDE, hk, dk).transpose(0, 3, 1, 2, 4).reshape(bsz, hk, n, CMP_STRIDE * dk)
    pe8 = jnp.broadcast_to(pe.reshape(1, CMP_BLOCK * dk), (8, CMP_BLOCK * dk))
    w2p = jnp.pad(w2, ((0, 0), (0, LANES - dk))).astype(BF16)
    o_shape = (bsz, hk, LANES, n) if transpose_out else (bsz, hk, n, LANES)
    const = lambda b, h: (0, 0)
    return pl.pallas_call(
        functools.partial(_compress_kernel, transpose_out=transpose_out),
        grid=(bsz, hk),
        in_specs=[pl.BlockSpec((None, None, n, CMP_STRIDE * dk), lambda b, h: (b, h, 0, 0)),
                  pl.BlockSpec(pe8.shape, const),
                  pl.BlockSpec(w1.shape, const),
                  pl.BlockSpec(w2p.shape, const)],
        out_specs=pl.BlockSpec((None, None) + o_shape[2:], lambda b, h: (b, h, 0, 0)),
        out_shape=jax.ShapeDtypeStruct(o_shape, BF16),
        compiler_params=_params("parallel", "parallel"),
        name="compress",
    )(xr, pe8, w1.astype(BF16), w2p)


def _importance_matrix(n_cmp_rows, n_blocks):
    c = np.arange(n_cmp_rows)[:, None]
    j = np.arange(n_blocks)[None, :]
    per = SEL_BLOCK // CMP_STRIDE
    a = ((c >= per * j) & (c <= per * j + per - 1)).astype(np.float32)
    for r in range(1, CMP_BLOCK // CMP_STRIDE):
        a += ((c + r >= per * j) & (c + r <= per * j + per - 1)).astype(np.float32)
    return a


def _dot_split3(a, w):
    out = None
    for _ in range(3):
        part = a.astype(BF16)
        a = a - part.astype(F32)
        term = jnp.dot(part, w, preferred_element_type=F32)
        out = term if out is None else out + term
    return out


def _top_selection_bias(imp, blk0):
    nb = imp.shape[1]
    tl = lax.broadcasted_iota(jnp.int32, (Q_BLOCK, nb), 0)
    jb = lax.broadcasted_iota(jnp.int32, (Q_BLOCK, nb), 1)
    qblk = blk0 + tl // SEL_BLOCK
    forced = (jb == 0) | (jb == qblk) | (jb == qblk - 1)
    score = jnp.where(forced, jnp.inf, jnp.where(jb <= qblk, imp, -1.0))
    sc_t = score.T

    def count_ge(v):
        return jnp.sum((sc_t >= v).astype(F32), axis=0, keepdims=True)

    top = jnp.max(sc_t, axis=0, keepdims=True)
    tau = top
    done = count_ge(top) >= N_SEL
    for _ in range(N_SEL - 1):
        top = jnp.max(jnp.where(sc_t < top, sc_t, -2.0), axis=0, keepdims=True)
        tau = jnp.where(done, tau, top)
        done = done | (count_ge(top) >= N_SEL)
    above = sc_t > tau
    tied = sc_t == tau
    room = N_SEL - jnp.sum(above.astype(F32), axis=0, keepdims=True)
    blk_r = lax.broadcasted_iota(jnp.int32, (nb, nb), 0)
    blk_c = lax.broadcasted_iota(jnp.int32, (nb, nb), 1)
    ties_before = jnp.dot((blk_c < blk_r).astype(BF16), tied.astype(BF16), preferred_element_type=F32)
    chosen = above | (tied & (ties_before < room))
    blk_t = lax.broadcasted_iota(jnp.int32, sc_t.shape, 0)
    qblk_t = blk0 + lax.broadcasted_iota(jnp.int32, sc_t.shape, 1) // SEL_BLOCK
    return jnp.where(chosen & (blk_t <= qblk_t), 0.0, -MASK_BIAS).T.astype(BF16)


def _nsa_kernel(qc_ref, qr_ref, gt_ref, kct_ref, vca_ref, ksa_ref, vsa_ref, kwt_ref, vwa_ref, imp_ref,
                o_ref, sc_ref, *, n_qb):
    i0 = pl.program_id(2) * n_qb
    rows_qb = NSA_GROUP * Q_BLOCK
    rows = n_qb * rows_qb
    n_cmp = kct_ref.shape[1]
    qc = qc_ref[...].reshape(rows, LANES)
    qr = qr_ref[...].reshape(rows, LANES)
    row = lax.broadcasted_iota(jnp.int32, (rows, 1), 0)
    tok = row % Q_BLOCK
    t = (i0 + row // rows_qb) * Q_BLOCK + tok

    s = jnp.dot(qc, kct_ref[...], preferred_element_type=F32)
    cmp_end = lax.broadcasted_iota(jnp.int32, (1, n_cmp), 1) * CMP_STRIDE + (CMP_BLOCK - 1)
    s = jnp.where(cmp_end <= t, s, -jnp.inf)
    m = jnp.max(s, axis=-1, keepdims=True)
    m = jnp.where(m == -jnp.inf, 0.0, m)
    p = jnp.exp2(s - m)
    d = jnp.sum(p, axis=-1, keepdims=True)
    p = p * (1.0 / jnp.where(d > 0, d, 1.0))
    o_cmp = jnp.dot(p.astype(BF16), vca_ref[...], preferred_element_type=F32)

    cw = lax.broadcasted_iota(jnp.int32, (1, WIN_UNITS * Q_BLOCK), 1)
    tok_qb = tok[0:rows_qb]
    o_win = []
    for qb in range(n_qb):
        iq = i0 + qb
        units = [jnp.maximum(iq - (WIN_UNITS - 1) + u, 0) for u in range(WIN_UNITS)]
        kw = jnp.concatenate([kwt_ref[u] for u in units], axis=1)
        vw = jnp.concatenate([vwa_ref[u] for u in units], axis=0)
        sw = jnp.dot(qr[qb * rows_qb:(qb + 1) * rows_qb], kw, preferred_element_type=F32)
        wmask = (cw <= tok_qb + WINDOW) & (cw > tok_qb) & (cw // Q_BLOCK + iq >= WIN_UNITS - 1)
        sw = jnp.where(wmask, sw, -jnp.inf)
        pw = jnp.exp2(sw.astype(BF16) - jnp.max(sw, axis=-1, keepdims=True).astype(BF16))
        acc_w = jnp.dot(pw, vw, preferred_element_type=F32)
        o_win.append(acc_w * (1.0 / acc_w[:, NSA_HEAD_DIM:NSA_HEAD_DIM + 1]))
    o_win = jnp.concatenate(o_win, axis=0)

    biases = []
    for qb in range(n_qb):
        base = qb * rows_qb
        p_grp = p[base:base + Q_BLOCK]
        for g in range(1, NSA_GROUP):
            p_grp = p_grp + p[base + g * Q_BLOCK:base + (g + 1) * Q_BLOCK]
        imp = _dot_split3(p_grp, imp_ref[...])
        bias = _top_selection_bias(imp, (i0 + qb) * (Q_BLOCK // SEL_BLOCK))
        biases += [bias] * NSA_GROUP
    q_aug = jnp.concatenate([jnp.concatenate(biases, axis=0), qr], axis=1)

    def scores(ck, slot):
        sc = jnp.dot(q_aug, ksa_ref[ck], preferred_element_type=F32)
        sc_ref[slot] = sc
        return jnp.max(sc, axis=-1, keepdims=True)

    def absorb(sc, sc_max, ck, m_run, acc):
        m_new = jnp.maximum(m_run, sc_max)
        pp = jnp.exp2(sc.astype(BF16) - m_new.astype(BF16))
        acc = jnp.exp2(m_run - m_new) * acc + jnp.dot(pp, vsa_ref[ck], preferred_element_type=F32)
        return m_new, acc

    last = (i0 * Q_BLOCK) // KEY_CHUNK

    def pair_body(kk, carry):
        mx0, m_run, acc = carry
        mx1 = scores(2 * kk + 1, 1)
        m_run, acc = absorb(sc_ref[0], mx0, 2 * kk, m_run, acc)
        mx0 = scores(2 * kk + 2, 0)
        m_run, acc = absorb(sc_ref[1], mx1, 2 * kk + 1, m_run, acc)
        return mx0, m_run, acc

    def odd_body(_, carry):
        mx0, m_run, acc = carry
        mx1 = scores(last, 1)
        m_run, acc = absorb(sc_ref[0], mx0, last - 1, m_run, acc)
        return mx1, m_run, acc

    carry = (scores(0, 0), jnp.full((rows, 1), -jnp.inf, F32), jnp.zeros((rows, LANES), F32))
    carry = lax.fori_loop(0, last // 2, pair_body, carry)
    _, m_run, acc_s = lax.fori_loop(0, last % 2, odd_body, carry)
    key_pos = last * KEY_CHUNK + lax.broadcasted_iota(jnp.int32, (1, KEY_CHUNK), 1)
    sc = jnp.where(key_pos <= t, sc_ref[last % 2], -jnp.inf)
    _, acc_s = absorb(sc, jnp.max(sc, axis=-1, keepdims=True), last, m_run, acc_s)
    o_sel = acc_s * (1.0 / acc_s[:, NSA_HEAD_DIM:NSA_HEAD_DIM + 1])

    gates = jax.nn.sigmoid(gt_ref[...])
    lane_lo = lax.broadcasted_iota(jnp.int32, (1, LANES), 1) < NSA_HEAD_DIM
    for qb in range(n_qb):
        gq = gates[qb * Q_BLOCK:(qb + 1) * Q_BLOCK]
        heads = []
        for g in range(NSA_GROUP):
            rs = slice(qb * rows_qb + g * Q_BLOCK, qb * rows_qb + (g + 1) * Q_BLOCK)
            c0 = g * N_BRANCH
            heads.append(gq[:, c0:c0 + 1] * o_cmp[rs] + gq[:, c0 + 1:c0 + 2] * o_sel[rs]
                         + gq[:, c0 + 2:c0 + 3] * o_win[rs])
        pairs = [jnp.where(lane_lo, heads[2 * k], pltpu.roll(heads[2 * k + 1], NSA_HEAD_DIM, axis=1))
                 for k in range(NSA_GROUP // 2)]
        o_ref[qb * Q_BLOCK:(qb + 1) * Q_BLOCK, :] = jnp.concatenate(pairs, axis=1).astype(o_ref.dtype)


def _nsa_attention(qc, qr, gates_raw, kc_t, vc_aug, ks_aug, vs_aug, kw_t, vw_aug, imp_mat, n_qb=2):
    bsz, nq, hk, rows_qb, _ = qc.shape
    s_len = nq * Q_BLOCK
    n_ch = ks_aug.shape[2]
    n_units = kw_t.shape[2]
    n_cmp = kc_t.shape[3]
    assert nq % n_qb == 0 and (KEY_CHUNK // Q_BLOCK) % n_qb == 0
    qspec = pl.BlockSpec((None, n_qb, None, rows_qb, LANES), lambda b, h, i: (b, i, h, 0, 0))
    per_head = lambda shp: pl.BlockSpec((None, None) + shp, lambda b, h, i: (b, h) + (0,) * len(shp))
    return pl.pallas_call(
        functools.partial(_nsa_kernel, n_qb=n_qb),
        grid=(bsz, hk, nq // n_qb),
        in_specs=[qspec, qspec,
                  pl.BlockSpec((None, n_qb * Q_BLOCK, LANES), lambda b, h, i: (b, i, h)),
                  per_head((LANES, n_cmp)),
                  per_head((n_cmp, LANES)),
                  per_head((n_ch, 2 * LANES, KEY_CHUNK)),
                  per_head((n_ch, KEY_CHUNK, LANES)),
                  per_head((n_units, LANES, Q_BLOCK)),
                  per_head((n_units, Q_BLOCK, LANES)),
                  pl.BlockSpec(imp_mat.shape, lambda b, h, i: (0, 0))],
        out_specs=pl.BlockSpec((None, n_qb * Q_BLOCK, NSA_GROUP * NSA_HEAD_DIM), lambda b, h, i: (b, i, h)),
        out_shape=jax.ShapeDtypeStruct((bsz, s_len, hk * NSA_GROUP * NSA_HEAD_DIM), BF16),
        scratch_shapes=[pltpu.VMEM((2, n_qb * rows_qb, KEY_CHUNK), F32)],
        compiler_params=_params("parallel", "parallel", "arbitrary"),
        name="nsa_attention",
    )(qc, qr, gates_raw, kc_t, vc_aug, ks_aug, vs_aug, kw_t, vw_aug, imp_mat)


def kernel(x, c, mod_w, mod_b, norm_pre_mix, norm_post_mix, norm_pre_ffn, norm_post_ffn, ffn_w_in, ffn_w_out, ssm_w_in, ssm_conv_w, ssm_conv_b, ssm_dt_bias, ssm_a_log, ssm_d, ssm_norm, ssm_w_out, kv_norm, kv_mod_w, kv_mod_b, w_kv, cmp_pos_k, cmp_w1_k, cmp_w2_k, cmp_pos_v, cmp_w1_v, cmp_w2_v, nsa_w_q, nsa_w_o):
    bsz, s_len, d = x.shape
    depth = mod_w.shape[0]
    n_a = ssm_w_in.shape[0]
    d_ff = ffn_w_out.shape[1]
    d_inner = ssm_w_out.shape[1]
    n_ssm_heads = d_inner // SSM_HEADDIM
    d_xbc = ssm_conv_w.shape[-1]
    dq = NSA_HEADS * NSA_HEAD_DIM
    dkv = NSA_KV_HEADS * NSA_HEAD_DIM
    assert s_len % KEY_CHUNK == 0 and s_len // SEL_BLOCK <= LANES and s_len // SEL_BLOCK >= N_SEL

    mods = _mods(c, mod_w, mod_b)
    kv_mods = _mods(c, kv_mod_w[None], kv_mod_b[None])[0]
    vec = lambda m, k: m[:, k * d:(k + 1) * d].reshape(bsz, 1, d)

    shared = None
    tables = None
    for i in range(depth):
        sh_m, sc_m, g_m, sh_f, sc_f, g_f = [vec(mods[i], k) for k in range(6)]
        if i == n_a:
            tables = _rope_tables(s_len)
            w_kv_b = w_kv.astype(BF16)
            kc, vc, ks, vs, kw, vw = _adaln_mm(x, kv_norm, vec(kv_mods, 0), vec(kv_mods, 1),
                                               [w_kv_b[:, k * dkv:(k + 1) * dkv] for k in range(6)])
            kc_t = _compress(kc, cmp_pos_k, cmp_w1_k, cmp_w2_k, transpose_out=True)
            vc_aug = _compress(vc, cmp_pos_v, cmp_w1_v, cmp_w2_v, transpose_out=False)
            shared = (kc_t, vc_aug) + _kv_post(ks, vs, kw, vw, tables)
            imp_mat = jnp.asarray(_importance_matrix(s_len // CMP_STRIDE, LANES), dtype=BF16)
        if i < n_a:
            w_in = ssm_w_in[i].astype(BF16)
            w_dt = jnp.pad(w_in[:, d_inner + d_xbc:], ((0, 0), (0, LANES - n_ssm_heads)))
            z, xbc, dt_raw = _adaln_mm(x, norm_pre_mix[i], sh_m, sc_m,
                                       [w_in[:, :d_inner], w_in[:, d_inner:d_inner + d_xbc], w_dt])
            y = _ssd(xbc, z, dt_raw, ssm_conv_w[i], ssm_conv_b[i], ssm_dt_bias[i], ssm_a_log[i],
                     ssm_d[i], ssm_norm[i])
            x = _mm_post(y, ssm_w_out[i].astype(BF16), x, g_m, norm_post_mix[i])
        else:
            w_q = nsa_w_q[i - n_a].astype(BF16)
            w_g = w_q[:, dq:].reshape(d, NSA_KV_HEADS, NSA_GROUP * N_BRANCH)
            w_g = jnp.pad(w_g, ((0, 0), (0, 0), (0, LANES - NSA_GROUP * N_BRANCH))).reshape(d, NSA_KV_HEADS * LANES)
            q, gates_raw = _adaln_mm(x, norm_pre_mix[i], sh_m, sc_m, [w_q[:, :dq], w_g])
            qc, qr = _q_post(q, tables)
            o = _nsa_attention(qc, qr, gates_raw, *shared, imp_mat)
            x = _mm_post(o, nsa_w_o[i - n_a].astype(BF16), x, g_m, norm_post_mix[i])
        w_ffn = ffn_w_in[i].astype(BF16)
        x = _ffn(x, norm_pre_ffn[i], sh_f, sc_f, w_ffn[:, :d_ff], w_ffn[:, d_ff:], ffn_w_out[i].astype(BF16),
                 norm_post_ffn[i], g_f)
    return x
```

```python
import functools
import math

import numpy as np
import jax
import jax.numpy as jnp
from jax import lax
from jax.experimental import pallas as pl
from jax.experimental.pallas import tpu as pltpu

F32 = jnp.float32
BF16 = jnp.bfloat16

EPS = 1e-6
DEPTH = 4
N_A = DEPTH // 2

SSM_HEADDIM = 64
SSM_STATE = 128
SSM_GROUPS = 4
CONV_W = 4
SSD_CHUNK = 256

NSA_HEADS = 16
NSA_KV_HEADS = 4
NSA_GROUP = NSA_HEADS // NSA_KV_HEADS
NSA_HEAD_DIM = 64
CMP_BLOCK = 32
CMP_STRIDE = 16
SEL_BLOCK = 64
N_SEL = 16
WINDOW = 512
Q_BLOCK = 128
N_BRANCH = 3
ROPE_THETA = 500000.0
ROT_DIMS = NSA_HEAD_DIM // 4

LANES = 128
KEY_CHUNK = 512
WIN_UNITS = WINDOW // Q_BLOCK + 1
MASK_BIAS = float(2 ** 30)
VMEM_LIMIT = 56 * 1024 * 1024


def _params(*sem):
    return pltpu.CompilerParams(dimension_semantics=sem, vmem_limit_bytes=VMEM_LIMIT)


def _rms(x, g):
    return x * lax.rsqrt(jnp.mean(x * x, axis=-1, keepdims=True) + EPS) * g


def _silu(x):
    return x * jax.nn.sigmoid(x)


def _mods_kernel(c_ref, w_ref, b_ref, o_ref):
    cs = _silu(c_ref[...])
    o_ref[...] = jnp.dot(cs, w_ref[...], preferred_element_type=F32) + b_ref[...]


def _mods(c, w, b):
    n_l, d, n = w.shape
    bsz = c.shape[0]
    tn = 1024
    return pl.pallas_call(
        _mods_kernel,
        grid=(n_l, n // tn),
        in_specs=[pl.BlockSpec((bsz, d), lambda l, j: (0, 0)),
                  pl.BlockSpec((None, d, tn), lambda l, j: (l, 0, j)),
                  pl.BlockSpec((None, 1, tn), lambda l, j: (l, 0, j))],
        out_specs=pl.BlockSpec((None, bsz, tn), lambda l, j: (l, 0, j)),
        out_shape=jax.ShapeDtypeStruct((n_l, bsz, n), F32),
        compiler_params=_params("parallel", "parallel"),
        name="mods",
    )(c, w, b.reshape(n_l, 1, n))


def _adaln_mm_kernel(x_ref, g_ref, sh_ref, sc_ref, *refs, n_w):
    h = _rms(x_ref[...], g_ref[...]) * (1.0 + sc_ref[...]) + sh_ref[...]
    hb = h.astype(BF16)
    for w_ref, o_ref in zip(refs[:n_w], refs[n_w:]):
        o_ref[...] = jnp.dot(hb, w_ref[...], preferred_element_type=F32).astype(o_ref.dtype)


def _adaln_mm(x, g, shift, scale, ws, tm=256):
    bsz, s_len, d = x.shape
    n_w = len(ws)
    const = lambda b, i: (0, 0)
    return pl.pallas_call(
        functools.partial(_adaln_mm_kernel, n_w=n_w),
        grid=(bsz, s_len // tm),
        in_specs=[pl.BlockSpec((None, tm, d), lambda b, i: (b, i, 0)),
                  pl.BlockSpec((1, d), const),
                  pl.BlockSpec((None, 1, d), lambda b, i: (b, 0, 0)),
                  pl.BlockSpec((None, 1, d), lambda b, i: (b, 0, 0))]
                 + [pl.BlockSpec(w.shape, const) for w in ws],
        out_specs=[pl.BlockSpec((None, tm, w.shape[1]), lambda b, i: (b, i, 0)) for w in ws],
        out_shape=[jax.ShapeDtypeStruct((bsz, s_len, w.shape[1]), F32) for w in ws],
        compiler_params=_params("parallel", "parallel"),
        name="adaln_mm",
    )(x, g.reshape(1, d), shift, scale, *ws)


def _mm_post_kernel(a_ref, w_ref, x_ref, gate_ref, g_ref, o_ref):
    y = jnp.dot(a_ref[...], w_ref[...], preferred_element_type=F32)
    o_ref[...] = x_ref[...] + gate_ref[...] * _rms(y, g_ref[...])


def _mm_post(a, w, x, gate, g, tm=512):
    bsz, s_len, d = x.shape
    k = a.shape[-1]
    return pl.pallas_call(
        _mm_post_kernel,
        grid=(bsz, s_len // tm),
        in_specs=[pl.BlockSpec((None, tm, k), lambda b, i: (b, i, 0)),
                  pl.BlockSpec((k, d), lambda b, i: (0, 0)),
                  pl.BlockSpec((None, tm, d), lambda b, i: (b, i, 0)),
                  pl.BlockSpec((None, 1, d), lambda b, i: (b, 0, 0)),
                  pl.BlockSpec((1, d), lambda b, i: (0, 0))],
        out_specs=pl.BlockSpec((None, tm, d), lambda b, i: (b, i, 0)),
        out_shape=jax.ShapeDtypeStruct(x.shape, F32),
        compiler_params=_params("parallel", "parallel"),
        name="mm_post",
    )(a, w, x, gate, g.reshape(1, d))


def _ffn_kernel(x_ref, g1_ref, sh_ref, sc_ref, wa_ref, wb_ref, wo_ref, g2_ref, gate_ref, o_ref, u_ref, *, tc):
    x = x_ref[...]
    hb = (_rms(x, g1_ref[...]) * (1.0 + sc_ref[...]) + sh_ref[...]).astype(BF16)
    d_ff = wa_ref.shape[1]
    for c0 in range(0, d_ff, tc):
        a = jnp.dot(hb, wa_ref[:, c0:c0 + tc], preferred_element_type=F32)
        b = jnp.dot(hb, wb_ref[:, c0:c0 + tc], preferred_element_type=F32)
        u_ref[:, c0:c0 + tc] = (_silu(a) * b).astype(BF16)
    y = jnp.dot(u_ref[...], wo_ref[...], preferred_element_type=F32)
    o_ref[...] = x + gate_ref[...] * _rms(y, g2_ref[...])


def _ffn(x, g1, shift, scale, wa, wb, wo, g2, gate, tm=512, tc=256):
    bsz, s_len, d = x.shape
    d_ff = wa.shape[1]
    const = lambda b, i: (0, 0)
    vec = lambda b, i: (b, 0, 0)
    return pl.pallas_call(
        functools.partial(_ffn_kernel, tc=tc),
        grid=(bsz, s_len // tm),
        in_specs=[pl.BlockSpec((None, tm, d), lambda b, i: (b, i, 0)),
                  pl.BlockSpec((1, d), const),
                  pl.BlockSpec((None, 1, d), vec),
                  pl.BlockSpec((None, 1, d), vec),
                  pl.BlockSpec((d, d_ff), const),
                  pl.BlockSpec((d, d_ff), const),
                  pl.BlockSpec((d_ff, d), const),
                  pl.BlockSpec((1, d), const),
                  pl.BlockSpec((None, 1, d), vec)],
        out_specs=pl.BlockSpec((None, tm, d), lambda b, i: (b, i, 0)),
        out_shape=jax.ShapeDtypeStruct(x.shape, F32),
        scratch_shapes=[pltpu.VMEM((tm, d_ff), BF16)],
        compiler_params=_params("parallel", "parallel"),
        name="ffn",
    )(x, g1.reshape(1, d), shift, scale, wa, wb, wo, g2.reshape(1, d), gate)


def _pair(mat, p, lane_lo):
    return jnp.where(lane_lo, mat[:, 2 * p:2 * p + 1], mat[:, 2 * p + 1:2 * p + 2])


def _ssd_kernel(xbc_ref, z_ref, dt_ref, cw_ref, cb_ref, dtb_ref, alog_ref, dsk_ref, ng_ref, o_ref,
                ext_ref, xs_ref, state_ref, y_ref, *, d_inner, n_groups):
    L = SSD_CHUNK
    N = SSM_STATE
    gw = d_inner // n_groups
    pairs_per_group = gw // LANES
    n_pairs = d_inner // LANES
    c_idx = pl.program_id(1)

    @pl.when(c_idx == 0)
    def _():
        ext_ref[0:8, :] = jnp.zeros((8, ext_ref.shape[1]), F32)
        state_ref[...] = jnp.zeros(state_ref.shape, F32)

    ext_ref[8:8 + L, :] = xbc_ref[...]
    d_xbc = ext_ref.shape[1]
    bc = []
    for c0 in range(0, d_xbc, 512):
        acc = cb_ref[:, c0:c0 + 512] + cw_ref[0:1, c0:c0 + 512] * ext_ref[5:5 + L, c0:c0 + 512]
        for k in range(1, CONV_W):
            acc = acc + cw_ref[k:k + 1, c0:c0 + 512] * ext_ref[5 + k:5 + k + L, c0:c0 + 512]
        acc = _silu(acc)
        if c0 < d_inner:
            xs_ref[:, c0:c0 + 512] = acc
        else:
            bc.append(acc)
    ext_ref[0:8, :] = ext_ref[L:L + 8, :]
    b_all = jnp.concatenate(bc[:len(bc) // 2], axis=1)
    c_all = jnp.concatenate(bc[len(bc) // 2:], axis=1)

    x_raw = dt_ref[...] + dtb_ref[...]
    dtv = jnp.maximum(x_raw, 0.0) + jnp.log1p(jnp.exp(-jnp.abs(x_raw)))
    da = dtv * (-jnp.exp(alog_ref[...]))
    ri = lax.broadcasted_iota(jnp.int32, (L, L), 0)
    ci = lax.broadcasted_iota(jnp.int32, (L, L), 1)
    causal = ci <= ri
    acs = jnp.dot(causal.astype(F32), da, preferred_element_type=F32, precision=lax.Precision.HIGHEST)
    acs_t = acs.T
    acs_last = acs[L - 1:L, :]
    e_acs = jnp.exp(acs)
    tail = jnp.exp(acs_last - acs)
    e_last = jnp.exp(acs_last)
    lane_lo = lax.broadcasted_iota(jnp.int32, (1, LANES), 1) < SSM_HEADDIM

    cb = None
    for p in range(n_pairs):
        g = p // pairs_per_group
        if p % pairs_per_group == 0:
            bg = b_all[:, g * N:(g + 1) * N]
            cg = c_all[:, g * N:(g + 1) * N].astype(BF16)
            cb = lax.dot_general(cg, bg.astype(BF16), (((1,), (1,)), ((), ())), preferred_element_type=F32)
            bg_t = bg.T.astype(BF16)
        x2 = xs_ref[:, p * LANES:(p + 1) * LANES]
        xdt = x2 * _pair(dtv, p, lane_lo)
        y2 = None
        for half in range(2):
            h = 2 * p + half
            seg = acs[:, h:h + 1] - acs_t[h:h + 1, :]
            m = (cb * jnp.exp(jnp.where(causal, seg, -jnp.inf))).astype(BF16)
            keep = lane_lo if half == 0 else jnp.logical_not(lane_lo)
            part = jnp.dot(m, jnp.where(keep, xdt, 0.0).astype(BF16), preferred_element_type=F32)
            y2 = part if y2 is None else y2 + part
        st = state_ref[:, p * LANES:(p + 1) * LANES]
        y2 = y2 + jnp.dot(cg, st.astype(BF16), preferred_element_type=F32) * _pair(e_acs, p, lane_lo)
        y2 = y2 + dsk_ref[:, p * LANES:(p + 1) * LANES] * x2
        y_ref[:, p * LANES:(p + 1) * LANES] = y2
        w2 = (xdt * _pair(tail, p, lane_lo)).astype(BF16)
        state_ref[:, p * LANES:(p + 1) * LANES] = (
            st * _pair(e_last, p, lane_lo) + jnp.dot(bg_t, w2, preferred_element_type=F32))

    for g in range(n_groups):
        sl = slice(g * gw, (g + 1) * gw)
        yg = y_ref[:, sl] * _silu(z_ref[:, sl])
        o_ref[:, sl] = _rms(yg, ng_ref[:, sl]).astype(o_ref.dtype)


def _ssd(xbc, z, dt_raw, conv_w, conv_b, dt_bias, a_log, d_skip, norm_g):
    bsz, s_len, d_xbc = xbc.shape
    d_inner = z.shape[-1]
    n_heads = d_inner // SSM_HEADDIM
    L = SSD_CHUNK
    pad = LANES - n_heads
    row = lambda v: jnp.pad(v.astype(F32), (0, pad)).reshape(1, LANES)
    const = lambda b, c: (0, 0)
    blk = lambda w: pl.BlockSpec((None, L, w), lambda b, c: (b, c, 0))
    return pl.pallas_call(
        functools.partial(_ssd_kernel, d_inner=d_inner, n_groups=SSM_GROUPS),
        grid=(bsz, s_len // L),
        in_specs=[blk(d_xbc), blk(d_inner), blk(LANES),
                  pl.BlockSpec((CONV_W, d_xbc), const),
                  pl.BlockSpec((1, d_xbc), const),
                  pl.BlockSpec((1, LANES), const),
                  pl.BlockSpec((1, LANES), const),
                  pl.BlockSpec((1, d_inner), const),
                  pl.BlockSpec((1, d_inner), const)],
        out_specs=blk(d_inner),
        out_shape=jax.ShapeDtypeStruct((bsz, s_len, d_inner), BF16),
        scratch_shapes=[pltpu.VMEM((L + 8, d_xbc), F32),
                        pltpu.VMEM((L, d_inner), F32),
                        pltpu.VMEM((SSM_STATE, d_inner), F32),
                        pltpu.VMEM((L, d_inner), F32)],
        compiler_params=_params("parallel", "arbitrary"),
        name="ssd",
    )(xbc, z, dt_raw, conv_w, conv_b.reshape(1, d_xbc), row(dt_bias), row(a_log),
      jnp.repeat(d_skip.astype(F32), SSM_HEADDIM).reshape(1, d_inner), norm_g.reshape(1, d_inner))


def _rope_tables(s_len):
    half = ROT_DIMS // 2
    inv_freq = ROPE_THETA ** (-jnp.arange(half, dtype=F32) / half)
    ang = jnp.arange(s_len).astype(F32)[:, None] * inv_freq[None, :]
    cos, sin = jnp.cos(ang), jnp.sin(ang)
    rest = NSA_HEAD_DIM - ROT_DIMS
    one = jnp.ones((s_len, rest), F32)
    zero = jnp.zeros((s_len, rest), F32)
    zh = jnp.zeros((s_len, half), F32)
    t_cos = jnp.concatenate([cos, cos, one], axis=1)
    t_up = jnp.concatenate([-sin, zh, zero], axis=1)
    t_dn = jnp.concatenate([zh, sin, zero], axis=1)
    two = lambda t: jnp.concatenate([t, t], axis=1)
    return two(t_cos), two(t_up), two(t_dn)


def _rope(x, t_cos, t_up, t_dn):
    half = ROT_DIMS // 2
    return x * t_cos + pltpu.roll(x, LANES - half, axis=1) * t_up + pltpu.roll(x, half, axis=1) * t_dn


def _q_post_kernel(q_ref, tc_ref, tu_ref, td_ref, qc_ref, qr_ref):
    scale = NSA_HEAD_DIM ** -0.5 * math.log2(math.e)
    lane_lo = lax.broadcasted_iota(jnp.int32, (1, LANES), 1) < NSA_HEAD_DIM
    for blk in range(qc_ref.shape[0]):
        rs = slice(blk * Q_BLOCK, (blk + 1) * Q_BLOCK)
        for col in range(q_ref.shape[1] // LANES):
            x = q_ref[rs, col * LANES:(col + 1) * LANES] * scale
            xr = _rope(x, tc_ref[rs, :], tu_ref[rs, :], td_ref[rs, :])
            for half in range(2):
                head = 2 * col + half
                kvh, g = head // NSA_GROUP, head % NSA_GROUP
                for src, dst in ((x, qc_ref), (xr, qr_ref)):
                    v = src if half == 0 else pltpu.roll(src, NSA_HEAD_DIM, axis=1)
                    dst[blk, kvh, g * Q_BLOCK:(g + 1) * Q_BLOCK, :] = jnp.where(lane_lo, v, 0.0).astype(BF16)


def _q_post(q, tables, n_blk=4):
    bsz, s_len, dq = q.shape
    nq = s_len // Q_BLOCK
    tm = n_blk * Q_BLOCK
    tab = pl.BlockSpec((tm, LANES), lambda b, i: (i, 0))
    out = pl.BlockSpec((None, n_blk, NSA_KV_HEADS, NSA_GROUP * Q_BLOCK, LANES), lambda b, i: (b, i, 0, 0, 0))
    shape = jax.ShapeDtypeStruct((bsz, nq, NSA_KV_HEADS, NSA_GROUP * Q_BLOCK, LANES), BF16)
    return pl.pallas_call(
        _q_post_kernel,
        grid=(bsz, nq // n_blk),
        in_specs=[pl.BlockSpec((None, tm, dq), lambda b, i: (b, i, 0)), tab, tab, tab],
        out_specs=[out, out],
        out_shape=[shape, shape],
        compiler_params=_params("parallel", "parallel"),
        name="q_post",
    )(q, *tables)


def _kv_post_kernel(ks_ref, vs_ref, kw_ref, vw_ref, tc_ref, tu_ref, td_ref,
                    ksa_ref, vsa_ref, kwt_ref, vwa_ref):
    tm = ks_ref.shape[0]
    chunk = pl.program_id(1)
    lane = lax.broadcasted_iota(jnp.int32, (1, LANES), 1)
    lane_lo = lane < NSA_HEAD_DIM
    ones_col = (lane == NSA_HEAD_DIM).astype(F32)
    row_lo = lax.broadcasted_iota(jnp.int32, (LANES, 1), 0) < NSA_HEAD_DIM
    blk_row = lax.broadcasted_iota(jnp.int32, (LANES, tm), 0)
    key_blk = chunk * (tm // SEL_BLOCK) + lax.broadcasted_iota(jnp.int32, (LANES, tm), 1) // SEL_BLOCK
    expand = (blk_row == key_blk).astype(BF16)
    for col in range(ks_ref.shape[1] // LANES):
        sl = slice(col * LANES, (col + 1) * LANES)
        ks_t = _rope(ks_ref[:, sl], tc_ref[...], tu_ref[...], td_ref[...]).T
        kw_t = _rope(kw_ref[:, sl], tc_ref[...], tu_ref[...], td_ref[...]).T
        vs, vw = vs_ref[:, sl], vw_ref[:, sl]
        for half in range(2):
            h = 2 * col + half
            if half == 1:
                ks_t = pltpu.roll(ks_t, NSA_HEAD_DIM, axis=0)
                kw_t = pltpu.roll(kw_t, NSA_HEAD_DIM, axis=0)
                vs = pltpu.roll(vs, NSA_HEAD_DIM, axis=1)
                vw = pltpu.roll(vw, NSA_HEAD_DIM, axis=1)
            ksa_ref[h, 0:LANES, :] = expand
            ksa_ref[h, LANES:2 * LANES, :] = jnp.where(row_lo, ks_t, 0.0).astype(BF16)
            kw_pad = jnp.where(row_lo, kw_t, 0.0).astype(BF16)
            for u in range(tm // Q_BLOCK):
                kwt_ref[h, u] = kw_pad[:, u * Q_BLOCK:(u + 1) * Q_BLOCK]
            vsa_ref[h] = jnp.where(lane_lo, vs, ones_col).astype(BF16)
            vwa_ref[h] = jnp.where(lane_lo, vw, ones_col).astype(BF16)


def _kv_post(ks, vs, kw, vw, tables):
    bsz, s_len, dk = ks.shape
    tm = KEY_CHUNK
    n_ch = s_len // tm
    upc = tm // Q_BLOCK
    inp = pl.BlockSpec((None, tm, dk), lambda b, c: (b, c, 0))
    tab = pl.BlockSpec((tm, LANES), lambda b, c: (c, 0))
    hk = NSA_KV_HEADS
    outs = pl.pallas_call(
        _kv_post_kernel,
        grid=(bsz, n_ch),
        in_specs=[inp, inp, inp, inp, tab, tab, tab],
        out_specs=[pl.BlockSpec((None, hk, None, 2 * LANES, tm), lambda b, c: (b, 0, c, 0, 0)),
                   pl.BlockSpec((None, hk, None, tm, LANES), lambda b, c: (b, 0, c, 0, 0)),
                   pl.BlockSpec((None, hk, upc, LANES, Q_BLOCK), lambda b, c: (b, 0, c, 0, 0)),
                   pl.BlockSpec((None, hk, None, tm, LANES), lambda b, c: (b, 0, c, 0, 0))],
        out_shape=[jax.ShapeDtypeStruct((bsz, hk, n_ch, 2 * LANES, tm), BF16),
                   jax.ShapeDtypeStruct((bsz, hk, n_ch, tm, LANES), BF16),
                   jax.ShapeDtypeStruct((bsz, hk, n_ch * upc, LANES, Q_BLOCK), BF16),
                   jax.ShapeDtypeStruct((bsz, hk, n_ch, tm, LANES), BF16)],
        compiler_params=_params("parallel", "parallel"),
        name="kv_post",
    )(ks, vs, kw, vw, *tables)
    ks_aug, vs_aug, kw_t, vw_aug = outs
    return ks_aug, vs_aug, kw_t, vw_aug.reshape(bsz, hk, n_ch * upc, Q_BLOCK, LANES)


def _compress_kernel(x_ref, pe_ref, w1_ref, w2_ref, o_ref, *, transpose_out):
    half = w1_ref.shape[0] // 2
    xb = x_ref[...].astype(BF16)
    y0 = jnp.dot(xb, w1_ref[0:half, :], preferred_element_type=F32)
    y1 = jnp.dot(xb, w1_ref[half:2 * half, :], preferred_element_type=F32)
    bias = jnp.dot(pe_ref[...].astype(BF16), w1_ref[...], preferred_element_type=F32)[0:1, :]
    n = y0.shape[0]
    hid = y0 + pltpu.roll(y1, n - 1, axis=0) + bias
    out = jnp.dot(_silu(hid).astype(BF16), w2_ref[...], preferred_element_type=F32)
    if transpose_out:
        o_ref[...] = out.T.astype(BF16)
    else:
        lane = lax.broadcasted_iota(jnp.int32, (1, LANES), 1)
        o_ref[...] = jnp.where(lane < NSA_HEAD_DIM, out, (lane == NSA_HEAD_DIM).astype(F32)).astype(BF16)


def _compress(t, pe, w1, w2, transpose_out):
    bsz, s_len, _ = t.shape
    hk, dk = NSA_KV_HEADS, NSA_HEAD_DIM
    n = s_len // CMP_STRIDE
    xr = t.reshape(bsz, n, CMP_STRIDE, hk, dk).transpose(0, 3, 1, 2, 4).reshape(bsz, hk, n, CMP_STRIDE * dk)
    pe8 = jnp.broadcast_to(pe.reshape(1, CMP_BLOCK * dk), (8, CMP_BLOCK * dk))
    w2p = jnp.pad(w2, ((0, 0), (0, LANES - dk))).astype(BF16)
    o_shape = (bsz, hk, LANES, n) if transpose_out else (bsz, hk, n, LANES)
    const = lambda b, h: (0, 0)
    return pl.pallas_call(
        functools.partial(_compress_kernel, transpose_out=transpose_out),
        grid=(bsz, hk),
        in_specs=[pl.BlockSpec((None, None, n, CMP_STRIDE * dk), lambda b, h: (b, h, 0, 0)),
                  pl.BlockSpec(pe8.shape, const),
                  pl.BlockSpec(w1.shape, const),
                  pl.BlockSpec(w2p.shape, const)],
        out_specs=pl.BlockSpec((None, None) + o_shape[2:], lambda b, h: (b, h, 0, 0)),
        out_shape=jax.ShapeDtypeStruct(o_shape, BF16),
        compiler_params=_params("parallel", "parallel"),
        name="compress",
    )(xr, pe8, w1.astype(BF16), w2p)


def _importance_matrix(n_cmp_rows, n_blocks):
    c = np.arange(n_cmp_rows)[:, None]
    j = np.arange(n_blocks)[None, :]
    per = SEL_BLOCK // CMP_STRIDE
    a = ((c >= per * j) & (c <= per * j + per - 1)).astype(np.float32)
    for r in range(1, CMP_BLOCK // CMP_STRIDE):
        a += ((c + r >= per * j) & (c + r <= per * j + per - 1)).astype(np.float32)
    return a


def _dot_split3(a, w):
    out = None
    for _ in range(3):
        part = a.astype(BF16)
        a = a - part.astype(F32)
        term = jnp.dot(part, w, preferred_element_type=F32)
        out = term if out is None else out + term
    return out


_SORT16 = ((0, 1), (2, 3), (0, 2), (1, 3), (1, 2), (4, 5), (6, 7), (4, 6), (5, 7), (5, 6), (0, 4), (2, 6), (2, 4),
           (1, 5), (3, 7), (3, 5), (1, 2), (3, 4), (5, 6), (8, 9), (10, 11), (8, 10), (9, 11), (9, 10), (12, 13),
           (14, 15), (12, 14), (13, 15), (13, 14), (8, 12), (10, 14), (10, 12), (9, 13), (11, 15), (11, 13), (9, 10),
           (11, 12), (13, 14), (0, 8), (4, 12), (4, 8), (2, 10), (6, 14), (6, 10), (2, 4), (6, 8), (10, 12), (1, 9),
           (5, 13), (5, 9), (3, 11), (7, 15), (7, 11), (3, 5), (7, 9), (11, 13), (1, 2), (3, 4), (5, 6), (7, 8),
           (9, 10), (11, 12), (13, 14))
_BITONIC16 = ((0, 8), (1, 9), (2, 10), (3, 11), (4, 12), (5, 13), (6, 14), (7, 15), (0, 4), (1, 5), (2, 6), (3, 7),
              (0, 2), (1, 3), (0, 1), (2, 3), (4, 6), (5, 7), (4, 5), (6, 7), (8, 12), (9, 13), (10, 14), (11, 15),
              (8, 10), (9, 11), (8, 9), (10, 11), (12, 14), (13, 15), (12, 13), (14, 15))


def _top_selection_bias(imp, blk0):
    nb = imp.shape[1]
    tl = lax.broadcasted_iota(jnp.int32, (Q_BLOCK, nb), 0)
    jb = lax.broadcasted_iota(jnp.int32, (Q_BLOCK, nb), 1)
    qblk = blk0 + tl // SEL_BLOCK
    forced = (jb == 0) | (jb == qblk) | (jb == qblk - 1)
    score = jnp.where(forced, jnp.inf, jnp.where(jb <= qblk, imp, -1.0))
    sc_t = score.T

    assert nb == N_SEL * 8
    v = [sc_t[8 * k:8 * k + 8] for k in range(N_SEL)]

    def exchange(net):
        for a_, b_ in net:
            hi, lo = jnp.maximum(v[a_], v[b_]), jnp.minimum(v[a_], v[b_])
            v[a_], v[b_] = hi, lo

    exchange(_SORT16)
    for shift in (4, 2):
        w = [pltpu.roll(x, shift, axis=0) for x in v]
        v = [jnp.maximum(v[k], w[N_SEL - 1 - k]) for k in range(N_SEL)]
        exchange(_BITONIC16)
    w = [pltpu.roll(x, 1, axis=0) for x in v]
    tau8 = jnp.maximum(v[0], w[N_SEL - 1])
    for k in range(1, N_SEL):
        tau8 = jnp.minimum(tau8, jnp.maximum(v[k], w[N_SEL - 1 - k]))
    tau = tau8[0:1]
    above = sc_t > tau
    tied = sc_t == tau
    room = N_SEL - jnp.sum(above.astype(F32), axis=0, keepdims=True)
    blk_r = lax.broadcasted_iota(jnp.int32, (nb, nb), 0)
    blk_c = lax.broadcasted_iota(jnp.int32, (nb, nb), 1)
    ties_before = jnp.dot((blk_c < blk_r).astype(BF16), tied.astype(BF16), preferred_element_type=F32)
    chosen = above | (tied & (ties_before < room))
    blk_t = lax.broadcasted_iota(jnp.int32, sc_t.shape, 0)
    qblk_t = blk0 + lax.broadcasted_iota(jnp.int32, sc_t.shape, 1) // SEL_BLOCK
    return jnp.where(chosen & (blk_t <= qblk_t), 0.0, -MASK_BIAS).T.astype(BF16)


def _nsa_kernel(qc_ref, qr_ref, gt_ref, kct_ref, vca_ref, ksa_ref, vsa_ref, kwt_ref, vwa_ref, imp_ref,
                o_ref, sc_ref, *, n_qb):
    i0 = pl.program_id(2) * n_qb
    rows_qb = NSA_GROUP * Q_BLOCK
    rows = n_qb * rows_qb
    n_cmp = kct_ref.shape[1]
    qc = qc_ref[...].reshape(rows, LANES)
    qr = qr_ref[...].reshape(rows, LANES)
    row = lax.broadcasted_iota(jnp.int32, (rows, 1), 0)
    tok = row % Q_BLOCK
    t = (i0 + row // rows_qb) * Q_BLOCK + tok

    s = jnp.dot(qc, kct_ref[...], preferred_element_type=F32)
    cmp_end = lax.broadcasted_iota(jnp.int32, (1, n_cmp), 1) * CMP_STRIDE + (CMP_BLOCK - 1)
    s = jnp.where(cmp_end <= t, s, -jnp.inf)
    m = jnp.max(s, axis=-1, keepdims=True)
    m = jnp.where(m == -jnp.inf, 0.0, m)
    p = jnp.exp2(s - m)
    d = jnp.sum(p, axis=-1, keepdims=True)
    p = p * (1.0 / jnp.where(d > 0, d, 1.0))
    o_cmp = jnp.dot(p.astype(BF16), vca_ref[...], preferred_element_type=F32)

    cw = lax.broadcasted_iota(jnp.int32, (1, WIN_UNITS * Q_BLOCK), 1)
    tok_qb = tok[0:rows_qb]
    o_win = []
    for qb in range(n_qb):
        iq = i0 + qb
        units = [jnp.maximum(iq - (WIN_UNITS - 1) + u, 0) for u in range(WIN_UNITS)]
        kw = jnp.concatenate([kwt_ref[u] for u in units], axis=1)
        vw = jnp.concatenate([vwa_ref[u] for u in units], axis=0)
        sw = jnp.dot(qr[qb * rows_qb:(qb + 1) * rows_qb], kw, preferred_element_type=F32)
        wmask = (cw <= tok_qb + WINDOW) & (cw > tok_qb) & (cw // Q_BLOCK + iq >= WIN_UNITS - 1)
        sw = jnp.where(wmask, sw, -jnp.inf)
        pw = jnp.exp2(sw.astype(BF16) - jnp.max(sw, axis=-1, keepdims=True).astype(BF16))
        acc_w = jnp.dot(pw, vw, preferred_element_type=F32)
        o_win.append(acc_w * (1.0 / acc_w[:, NSA_HEAD_DIM:NSA_HEAD_DIM + 1]))
    o_win = jnp.concatenate(o_win, axis=0)

    biases = []
    for qb in range(n_qb):
        base = qb * rows_qb
        p_grp = p[base:base + Q_BLOCK]
        for g in range(1, NSA_GROUP):
            p_grp = p_grp + p[base + g * Q_BLOCK:base + (g + 1) * Q_BLOCK]
        imp = _dot_split3(p_grp, imp_ref[...])
        bias = _top_selection_bias(imp, (i0 + qb) * (Q_BLOCK // SEL_BLOCK))
        biases += [bias] * NSA_GROUP
    q_aug = jnp.concatenate([jnp.concatenate(biases, axis=0), qr], axis=1)

    def scores(ck, slot):
        sc_ref[slot] = jnp.dot(q_aug, ksa_ref[ck], preferred_element_type=F32)

    def absorb(sc, ck, m_run, acc):
        m_new = jnp.maximum(m_run, jnp.max(sc, axis=-1, keepdims=True))
        pp = jnp.exp2(sc.astype(BF16) - m_new.astype(BF16))
        acc = jnp.exp2(m_run - m_new) * acc + jnp.dot(pp, vsa_ref[ck], preferred_element_type=F32)
        return m_new, acc

    last = (i0 * Q_BLOCK) // KEY_CHUNK

    def pair_body(kk, carry):
        scores(2 * kk + 1, 1)
        carry = absorb(sc_ref[0], 2 * kk, *carry)
        scores(2 * kk + 2, 0)
        return absorb(sc_ref[1], 2 * kk + 1, *carry)

    def odd_body(_, carry):
        scores(last, 1)
        return absorb(sc_ref[0], last - 1, *carry)

    scores(0, 0)
    carry = (jnp.full((rows, 1), -jnp.inf, F32), jnp.zeros((rows, LANES), F32))
    carry = lax.fori_loop(0, last // 2, pair_body, carry)
    carry = lax.fori_loop(0, last % 2, odd_body, carry)
    key_pos = last * KEY_CHUNK + lax.broadcasted_iota(jnp.int32, (1, KEY_CHUNK), 1)
    _, acc_s = absorb(jnp.where(key_pos <= t, sc_ref[last % 2], -jnp.inf), last, *carry)
    o_sel = acc_s * (1.0 / acc_s[:, NSA_HEAD_DIM:NSA_HEAD_DIM + 1])

    gates = jax.nn.sigmoid(gt_ref[...])
    lane_lo = lax.broadcasted_iota(jnp.int32, (1, LANES), 1) < NSA_HEAD_DIM
    for qb in range(n_qb):
        gq = gates[qb * Q_BLOCK:(qb + 1) * Q_BLOCK]
        heads = []
        for g in range(NSA_GROUP):
            rs = slice(qb * rows_qb + g * Q_BLOCK, qb * rows_qb + (g + 1) * Q_BLOCK)
            c0 = g * N_BRANCH
            heads.append(gq[:, c0:c0 + 1] * o_cmp[rs] + gq[:, c0 + 1:c0 + 2] * o_sel[rs]
                         + gq[:, c0 + 2:c0 + 3] * o_win[rs])
        pairs = [jnp.where(lane_lo, heads[2 * k], pltpu.roll(heads[2 * k + 1], NSA_HEAD_DIM, axis=1))
                 for k in range(NSA_GROUP // 2)]
        o_ref[qb * Q_BLOCK:(qb + 1) * Q_BLOCK, :] = jnp.concatenate(pairs, axis=1).astype(o_ref.dtype)


def _nsa_attention(qc, qr, gates_raw, kc_t, vc_aug, ks_aug, vs_aug, kw_t, vw_aug, imp_mat, n_qb=2):
    bsz, nq, hk, rows_qb, _ = qc.shape
    s_len = nq * Q_BLOCK
    n_ch = ks_aug.shape[2]
    n_units = kw_t.shape[2]
    n_cmp = kc_t.shape[3]
    assert nq % n_qb == 0 and (KEY_CHUNK // Q_BLOCK) % n_qb == 0
    qspec = pl.BlockSpec((None, n_qb, None, rows_qb, LANES), lambda b, h, i: (b, i, h, 0, 0))
    per_head = lambda shp: pl.BlockSpec((None, None) + shp, lambda b, h, i: (b, h) + (0,) * len(shp))
    return pl.pallas_call(
        functools.partial(_nsa_kernel, n_qb=n_qb),
        grid=(bsz, hk, nq // n_qb),
        in_specs=[qspec, qspec,
                  pl.BlockSpec((None, n_qb * Q_BLOCK, LANES), lambda b, h, i: (b, i, h)),
                  per_head((LANES, n_cmp)),
                  per_head((n_cmp, LANES)),
                  per_head((n_ch, 2 * LANES, KEY_CHUNK)),
                  per_head((n_ch, KEY_CHUNK, LANES)),
                  per_head((n_units, LANES, Q_BLOCK)),
                  per_head((n_units, Q_BLOCK, LANES)),
                  pl.BlockSpec(imp_mat.shape, lambda b, h, i: (0, 0))],
        out_specs=pl.BlockSpec((None, n_qb * Q_BLOCK, NSA_GROUP * NSA_HEAD_DIM), lambda b, h, i: (b, i, h)),
        out_shape=jax.ShapeDtypeStruct((bsz, s_len, hk * NSA_GROUP * NSA_HEAD_DIM), BF16),
        scratch_shapes=[pltpu.VMEM((2, n_qb * rows_qb, KEY_CHUNK), F32)],
        compiler_params=_params("parallel", "parallel", "arbitrary"),
        name="nsa_attention",
    )(qc, qr, gates_raw, kc_t, vc_aug, ks_aug, vs_aug, kw_t, vw_aug, imp_mat)


def kernel(x, c, mod_w, mod_b, norm_pre_mix, norm_post_mix, norm_pre_ffn, norm_post_ffn, ffn_w_in, ffn_w_out, ssm_w_in, ssm_conv_w, ssm_conv_b, ssm_dt_bias, ssm_a_log, ssm_d, ssm_norm, ssm_w_out, kv_norm, kv_mod_w, kv_mod_b, w_kv, cmp_pos_k, cmp_w1_k, cmp_w2_k, cmp_pos_v, cmp_w1_v, cmp_w2_v, nsa_w_q, nsa_w_o):
    bsz, s_len, d = x.shape
    depth = mod_w.shape[0]
    n_a = ssm_w_in.shape[0]
    d_ff = ffn_w_out.shape[1]
    d_inner = ssm_w_out.shape[1]
    n_ssm_heads = d_inner // SSM_HEADDIM
    d_xbc = ssm_conv_w.shape[-1]
    dq = NSA_HEADS * NSA_HEAD_DIM
    dkv = NSA_KV_HEADS * NSA_HEAD_DIM
    assert s_len % KEY_CHUNK == 0 and s_len // SEL_BLOCK <= LANES and s_len // SEL_BLOCK >= N_SEL

    mods = _mods(c, mod_w, mod_b)
    kv_mods = _mods(c, kv_mod_w[None], kv_mod_b[None])[0]
    vec = lambda m, k: m[:, k * d:(k + 1) * d].reshape(bsz, 1, d)

    shared = None
    tables = None
    for i in range(depth):
        sh_m, sc_m, g_m, sh_f, sc_f, g_f = [vec(mods[i], k) for k in range(6)]
        if i == n_a:
            tables = _rope_tables(s_len)
            w_kv_b = w_kv.astype(BF16)
            kc, vc, ks, vs, kw, vw = _adaln_mm(x, kv_norm, vec(kv_mods, 0), vec(kv_mods, 1),
                                               [w_kv_b[:, k * dkv:(k + 1) * dkv] for k in range(6)])
            kc_t = _compress(kc, cmp_pos_k, cmp_w1_k, cmp_w2_k, transpose_out=True)
            vc_aug = _compress(vc, cmp_pos_v, cmp_w1_v, cmp_w2_v, transpose_out=False)
            shared = (kc_t, vc_aug) + _kv_post(ks, vs, kw, vw, tables)
            imp_mat = jnp.asarray(_importance_matrix(s_len // CMP_STRIDE, LANES), dtype=BF16)
        if i < n_a:
            w_in = ssm_w_in[i].astype(BF16)
            w_dt = jnp.pad(w_in[:, d_inner + d_xbc:], ((0, 0), (0, LANES - n_ssm_heads)))
            z, xbc, dt_raw = _adaln_mm(x, norm_pre_mix[i], sh_m, sc_m,
                                       [w_in[:, :d_inner], w_in[:, d_inner:d_inner + d_xbc], w_dt])
            y = _ssd(xbc, z, dt_raw, ssm_conv_w[i], ssm_conv_b[i], ssm_dt_bias[i], ssm_a_log[i],
                     ssm_d[i], ssm_norm[i])
            x = _mm_post(y, ssm_w_out[i].astype(BF16), x, g_m, norm_post_mix[i])
        else:
            w_q = nsa_w_q[i - n_a].astype(BF16)
            w_g = w_q[:, dq:].reshape(d, NSA_KV_HEADS, NSA_GROUP * N_BRANCH)
            w_g = jnp.pad(w_g, ((0, 0), (0, 0), (0, LANES - NSA_GROUP * N_BRANCH))).reshape(d, NSA_KV_HEADS * LANES)
            q, gates_raw = _adaln_mm(x, norm_pre_mix[i], sh_m, sc_m, [w_q[:, :dq], w_g])
            qc, qr = _q_post(q, tables)
            o = _nsa_attention(qc, qr, gates_raw, *shared, imp_mat)
            x = _mm_post(o, nsa_w_o[i - n_a].astype(BF16), x, g_m, norm_post_mix[i])
        w_ffn = ffn_w_in[i].astype(BF16)
        x = _ffn(x, norm_pre_ffn[i], sh_f, sc_f, w_ffn[:, :d_ff], w_ffn[:, d_ff:], ffn_w_out[i].astype(BF16),
                 norm_post_ffn[i], g_f)
    return x
```

```python
import functools
import math

import numpy as np
import jax
import jax.numpy as jnp
from jax import lax
from jax.experimental import pallas as pl
from jax.experimental.pallas import tpu as pltpu

F32 = jnp.float32
BF16 = jnp.bfloat16

EPS = 1e-6
DEPTH = 4
N_A = DEPTH // 2

SSM_HEADDIM = 64
SSM_STATE = 128
SSM_GROUPS = 4
CONV_W = 4
SSD_CHUNK = 256

NSA_HEADS = 16
NSA_KV_HEADS = 4
NSA_GROUP = NSA_HEADS // NSA_KV_HEADS
NSA_HEAD_DIM = 64
CMP_BLOCK = 32
CMP_STRIDE = 16
SEL_BLOCK = 64
N_SEL = 16
WINDOW = 512
Q_BLOCK = 128
N_BRANCH = 3
ROPE_THETA = 500000.0
ROT_DIMS = NSA_HEAD_DIM // 4

LANES = 128
KEY_CHUNK = 512
WIN_UNITS = WINDOW // Q_BLOCK + 1
MASK_BIAS = float(2 ** 30)
VMEM_LIMIT = 56 * 1024 * 1024


def _params(*sem):
    return pltpu.CompilerParams(dimension_semantics=sem, vmem_limit_bytes=VMEM_LIMIT)


def _rms(x, g):
    return x * lax.rsqrt(jnp.mean(x * x, axis=-1, keepdims=True) + EPS) * g


def _silu(x):
    return x * jax.nn.sigmoid(x)


def _mods_kernel(c_ref, w_ref, b_ref, o_ref):
    cs = _silu(c_ref[...])
    o_ref[...] = jnp.dot(cs, w_ref[...], preferred_element_type=F32) + b_ref[...]


def _mods(c, w, b):
    n_l, d, n = w.shape
    bsz = c.shape[0]
    tn = 1024
    return pl.pallas_call(
        _mods_kernel,
        grid=(n_l, n // tn),
        in_specs=[pl.BlockSpec((bsz, d), lambda l, j: (0, 0)),
                  pl.BlockSpec((None, d, tn), lambda l, j: (l, 0, j)),
                  pl.BlockSpec((None, 1, tn), lambda l, j: (l, 0, j))],
        out_specs=pl.BlockSpec((None, bsz, tn), lambda l, j: (l, 0, j)),
        out_shape=jax.ShapeDtypeStruct((n_l, bsz, n), F32),
        compiler_params=_params("parallel", "parallel"),
        name="mods",
    )(c, w, b.reshape(n_l, 1, n))


def _adaln_mm_kernel(x_ref, g_ref, sh_ref, sc_ref, *refs, n_w):
    h = _rms(x_ref[...], g_ref[...]) * (1.0 + sc_ref[...]) + sh_ref[...]
    hb = h.astype(BF16)
    for w_ref, o_ref in zip(refs[:n_w], refs[n_w:]):
        o_ref[...] = jnp.dot(hb, w_ref[...], preferred_element_type=F32).astype(o_ref.dtype)


def _adaln_mm(x, g, shift, scale, ws, tm=256):
    bsz, s_len, d = x.shape
    n_w = len(ws)
    const = lambda b, i: (0, 0)
    return pl.pallas_call(
        functools.partial(_adaln_mm_kernel, n_w=n_w),
        grid=(bsz, s_len // tm),
        in_specs=[pl.BlockSpec((None, tm, d), lambda b, i: (b, i, 0)),
                  pl.BlockSpec((1, d), const),
                  pl.BlockSpec((None, 1, d), lambda b, i: (b, 0, 0)),
                  pl.BlockSpec((None, 1, d), lambda b, i: (b, 0, 0))]
                 + [pl.BlockSpec(w.shape, const) for w in ws],
        out_specs=[pl.BlockSpec((None, tm, w.shape[1]), lambda b, i: (b, i, 0)) for w in ws],
        out_shape=[jax.ShapeDtypeStruct((bsz, s_len, w.shape[1]), F32) for w in ws],
        compiler_params=_params("parallel", "parallel"),
        name="adaln_mm",
    )(x, g.reshape(1, d), shift, scale, *ws)


def _mm_post_kernel(a_ref, w_ref, x_ref, gate_ref, g_ref, o_ref):
    y = jnp.dot(a_ref[...], w_ref[...], preferred_element_type=F32)
    o_ref[...] = x_ref[...] + gate_ref[...] * _rms(y, g_ref[...])


def _mm_post(a, w, x, gate, g, tm=512):
    bsz, s_len, d = x.shape
    k = a.shape[-1]
    return pl.pallas_call(
        _mm_post_kernel,
        grid=(bsz, s_len // tm),
        in_specs=[pl.BlockSpec((None, tm, k), lambda b, i: (b, i, 0)),
                  pl.BlockSpec((k, d), lambda b, i: (0, 0)),
                  pl.BlockSpec((None, tm, d), lambda b, i: (b, i, 0)),
                  pl.BlockSpec((None, 1, d), lambda b, i: (b, 0, 0)),
                  pl.BlockSpec((1, d), lambda b, i: (0, 0))],
        out_specs=pl.BlockSpec((None, tm, d), lambda b, i: (b, i, 0)),
        out_shape=jax.ShapeDtypeStruct(x.shape, F32),
        compiler_params=_params("parallel", "parallel"),
        name="mm_post",
    )(a, w, x, gate, g.reshape(1, d))


def _ffn_kernel(x_ref, g1_ref, sh_ref, sc_ref, wa_ref, wb_ref, wo_ref, g2_ref, gate_ref, o_ref, u_ref, *, tc):
    x = x_ref[...]
    hb = (_rms(x, g1_ref[...]) * (1.0 + sc_ref[...]) + sh_ref[...]).astype(BF16)
    d_ff = wa_ref.shape[1]
    for c0 in range(0, d_ff, tc):
        a = jnp.dot(hb, wa_ref[:, c0:c0 + tc], preferred_element_type=F32)
        b = jnp.dot(hb, wb_ref[:, c0:c0 + tc], preferred_element_type=F32)
        u_ref[:, c0:c0 + tc] = (_silu(a) * b).astype(BF16)
    y = jnp.dot(u_ref[...], wo_ref[...], preferred_element_type=F32)
    o_ref[...] = x + gate_ref[...] * _rms(y, g2_ref[...])


def _ffn(x, g1, shift, scale, wa, wb, wo, g2, gate, tm=512, tc=256):
    bsz, s_len, d = x.shape
    d_ff = wa.shape[1]
    const = lambda b, i: (0, 0)
    vec = lambda b, i: (b, 0, 0)
    return pl.pallas_call(
        functools.partial(_ffn_kernel, tc=tc),
        grid=(bsz, s_len // tm),
        in_specs=[pl.BlockSpec((None, tm, d), lambda b, i: (b, i, 0)),
                  pl.BlockSpec((1, d), const),
                  pl.BlockSpec((None, 1, d), vec),
                  pl.BlockSpec((None, 1, d), vec),
                  pl.BlockSpec((d, d_ff), const),
                  pl.BlockSpec((d, d_ff), const),
                  pl.BlockSpec((d_ff, d), const),
                  pl.BlockSpec((1, d), const),
                  pl.BlockSpec((None, 1, d), vec)],
        out_specs=pl.BlockSpec((None, tm, d), lambda b, i: (b, i, 0)),
        out_shape=jax.ShapeDtypeStruct(x.shape, F32),
        scratch_shapes=[pltpu.VMEM((tm, d_ff), BF16)],
        compiler_params=_params("parallel", "parallel"),
        name="ffn",
    )(x, g1.reshape(1, d), shift, scale, wa, wb, wo, g2.reshape(1, d), gate)


def _pair(mat, p, lane_lo):
    return jnp.where(lane_lo, mat[:, 2 * p:2 * p + 1], mat[:, 2 * p + 1:2 * p + 2])


def _ssd_kernel(xbc_ref, z_ref, dt_ref, cw_ref, cb_ref, dtb_ref, alog_ref, dsk_ref, ng_ref, o_ref,
                ext_ref, xs_ref, state_ref, y_ref, *, d_inner, n_groups):
    L = SSD_CHUNK
    N = SSM_STATE
    gw = d_inner // n_groups
    pairs_per_group = gw // LANES
    n_pairs = d_inner // LANES
    c_idx = pl.program_id(1)

    @pl.when(c_idx == 0)
    def _():
        ext_ref[0:8, :] = jnp.zeros((8, ext_ref.shape[1]), F32)
        state_ref[...] = jnp.zeros(state_ref.shape, F32)

    ext_ref[8:8 + L, :] = xbc_ref[...]
    d_xbc = ext_ref.shape[1]
    bc = []
    for c0 in range(0, d_xbc, 512):
        acc = cb_ref[:, c0:c0 + 512] + cw_ref[0:1, c0:c0 + 512] * ext_ref[5:5 + L, c0:c0 + 512]
        for k in range(1, CONV_W):
            acc = acc + cw_ref[k:k + 1, c0:c0 + 512] * ext_ref[5 + k:5 + k + L, c0:c0 + 512]
        acc = _silu(acc)
        if c0 < d_inner:
            xs_ref[:, c0:c0 + 512] = acc
        else:
            bc.append(acc)
    ext_ref[0:8, :] = ext_ref[L:L + 8, :]
    b_all = jnp.concatenate(bc[:len(bc) // 2], axis=1)
    c_all = jnp.concatenate(bc[len(bc) // 2:], axis=1)

    x_raw = dt_ref[...] + dtb_ref[...]
    dtv = jnp.maximum(x_raw, 0.0) + jnp.log1p(jnp.exp(-jnp.abs(x_raw)))
    da = dtv * (-jnp.exp(alog_ref[...]))
    ri = lax.broadcasted_iota(jnp.int32, (L, L), 0)
    ci = lax.broadcasted_iota(jnp.int32, (L, L), 1)
    causal = ci <= ri
    acs = jnp.dot(causal.astype(F32), da, preferred_element_type=F32, precision=lax.Precision.HIGHEST)
    acs_t = acs.T
    acs_last = acs[L - 1:L, :]
    e_acs = jnp.exp(acs)
    tail = jnp.exp(acs_last - acs)
    e_last = jnp.exp(acs_last)
    lane_lo = lax.broadcasted_iota(jnp.int32, (1, LANES), 1) < SSM_HEADDIM

    cb = None
    for p in range(n_pairs):
        g = p // pairs_per_group
        if p % pairs_per_group == 0:
            bg = b_all[:, g * N:(g + 1) * N]
            cg = c_all[:, g * N:(g + 1) * N].astype(BF16)
            cb = lax.dot_general(cg, bg.astype(BF16), (((1,), (1,)), ((), ())), preferred_element_type=F32)
            bg_t = bg.T.astype(BF16)
        x2 = xs_ref[:, p * LANES:(p + 1) * LANES]
        xdt = x2 * _pair(dtv, p, lane_lo)
        y2 = None
        for half in range(2):
            h = 2 * p + half
            seg = acs[:, h:h + 1] - acs_t[h:h + 1, :]
            m = (cb * jnp.exp(jnp.where(causal, seg, -jnp.inf))).astype(BF16)
            keep = lane_lo if half == 0 else jnp.logical_not(lane_lo)
            part = jnp.dot(m, jnp.where(keep, xdt, 0.0).astype(BF16), preferred_element_type=F32)
            y2 = part if y2 is None else y2 + part
        st = state_ref[:, p * LANES:(p + 1) * LANES]
        y2 = y2 + jnp.dot(cg, st.astype(BF16), preferred_element_type=F32) * _pair(e_acs, p, lane_lo)
        y2 = y2 + dsk_ref[:, p * LANES:(p + 1) * LANES] * x2
        y_ref[:, p * LANES:(p + 1) * LANES] = y2
        w2 = (xdt * _pair(tail, p, lane_lo)).astype(BF16)
        state_ref[:, p * LANES:(p + 1) * LANES] = (
            st * _pair(e_last, p, lane_lo) + jnp.dot(bg_t, w2, preferred_element_type=F32))

    for g in range(n_groups):
        sl = slice(g * gw, (g + 1) * gw)
        yg = y_ref[:, sl] * _silu(z_ref[:, sl])
        o_ref[:, sl] = _rms(yg, ng_ref[:, sl]).astype(o_ref.dtype)


def _ssd(xbc, z, dt_raw, conv_w, conv_b, dt_bias, a_log, d_skip, norm_g):
    bsz, s_len, d_xbc = xbc.shape
    d_inner = z.shape[-1]
    n_heads = d_inner // SSM_HEADDIM
    L = SSD_CHUNK
    pad = LANES - n_heads
    row = lambda v: jnp.pad(v.astype(F32), (0, pad)).reshape(1, LANES)
    const = lambda b, c: (0, 0)
    blk = lambda w: pl.BlockSpec((None, L, w), lambda b, c: (b, c, 0))
    return pl.pallas_call(
        functools.partial(_ssd_kernel, d_inner=d_inner, n_groups=SSM_GROUPS),
        grid=(bsz, s_len // L),
        in_specs=[blk(d_xbc), blk(d_inner), blk(LANES),
                  pl.BlockSpec((CONV_W, d_xbc), const),
                  pl.BlockSpec((1, d_xbc), const),
                  pl.BlockSpec((1, LANES), const),
                  pl.BlockSpec((1, LANES), const),
                  pl.BlockSpec((1, d_inner), const),
                  pl.BlockSpec((1, d_inner), const)],
        out_specs=blk(d_inner),
        out_shape=jax.ShapeDtypeStruct((bsz, s_len, d_inner), BF16),
        scratch_shapes=[pltpu.VMEM((L + 8, d_xbc), F32),
                        pltpu.VMEM((L, d_inner), F32),
                        pltpu.VMEM((SSM_STATE, d_inner), F32),
                        pltpu.VMEM((L, d_inner), F32)],
        compiler_params=_params("parallel", "arbitrary"),
        name="ssd",
    )(xbc, z, dt_raw, conv_w, conv_b.reshape(1, d_xbc), row(dt_bias), row(a_log),
      jnp.repeat(d_skip.astype(F32), SSM_HEADDIM).reshape(1, d_inner), norm_g.reshape(1, d_inner))


def _rope_tables(s_len):
    half = ROT_DIMS // 2
    inv_freq = ROPE_THETA ** (-jnp.arange(half, dtype=F32) / half)
    ang = jnp.arange(s_len).astype(F32)[:, None] * inv_freq[None, :]
    cos, sin = jnp.cos(ang), jnp.sin(ang)
    rest = NSA_HEAD_DIM - ROT_DIMS
    one = jnp.ones((s_len, rest), F32)
    zero = jnp.zeros((s_len, rest), F32)
    zh = jnp.zeros((s_len, half), F32)
    t_cos = jnp.concatenate([cos, cos, one], axis=1)
    t_up = jnp.concatenate([-sin, zh, zero], axis=1)
    t_dn = jnp.concatenate([zh, sin, zero], axis=1)
    two = lambda t: jnp.concatenate([t, t], axis=1)
    return two(t_cos), two(t_up), two(t_dn)


def _rope(x, t_cos, t_up, t_dn):
    half = ROT_DIMS // 2
    return x * t_cos + pltpu.roll(x, LANES - half, axis=1) * t_up + pltpu.roll(x, half, axis=1) * t_dn


def _q_post_kernel(q_ref, tc_ref, tu_ref, td_ref, qc_ref, qr_ref):
    scale = NSA_HEAD_DIM ** -0.5 * math.log2(math.e)
    lane_lo = lax.broadcasted_iota(jnp.int32, (1, LANES), 1) < NSA_HEAD_DIM
    for blk in range(qc_ref.shape[0]):
        rs = slice(blk * Q_BLOCK, (blk + 1) * Q_BLOCK)
        for col in range(q_ref.shape[1] // LANES):
            x = q_ref[rs, col * LANES:(col + 1) * LANES] * scale
            xr = _rope(x, tc_ref[rs, :], tu_ref[rs, :], td_ref[rs, :])
            for half in range(2):
                head = 2 * col + half
                kvh, g = head // NSA_GROUP, head % NSA_GROUP
                for src, dst in ((x, qc_ref), (xr, qr_ref)):
                    v = src if half == 0 else pltpu.roll(src, NSA_HEAD_DIM, axis=1)
                    dst[blk, kvh, g * Q_BLOCK:(g + 1) * Q_BLOCK, :] = jnp.where(lane_lo, v, 0.0).astype(BF16)


def _q_post(q, tables, n_blk=4):
    bsz, s_len, dq = q.shape
    nq = s_len // Q_BLOCK
    tm = n_blk * Q_BLOCK
    tab = pl.BlockSpec((tm, LANES), lambda b, i: (i, 0))
    out = pl.BlockSpec((None, n_blk, NSA_KV_HEADS, NSA_GROUP * Q_BLOCK, LANES), lambda b, i: (b, i, 0, 0, 0))
    shape = jax.ShapeDtypeStruct((bsz, nq, NSA_KV_HEADS, NSA_GROUP * Q_BLOCK, LANES), BF16)
    return pl.pallas_call(
        _q_post_kernel,
        grid=(bsz, nq // n_blk),
        in_specs=[pl.BlockSpec((None, tm, dq), lambda b, i: (b, i, 0)), tab, tab, tab],
        out_specs=[out, out],
        out_shape=[shape, shape],
        compiler_params=_params("parallel", "parallel"),
        name="q_post",
    )(q, *tables)


def _kv_post_kernel(ks_ref, vs_ref, kw_ref, vw_ref, tc_ref, tu_ref, td_ref,
                    ksa_ref, vsa_ref, kwt_ref, vwa_ref):
    tm = ks_ref.shape[0]
    chunk = pl.program_id(1)
    lane = lax.broadcasted_iota(jnp.int32, (1, LANES), 1)
    lane_lo = lane < NSA_HEAD_DIM
    ones_col = (lane == NSA_HEAD_DIM).astype(F32)
    row_lo = lax.broadcasted_iota(jnp.int32, (LANES, 1), 0) < NSA_HEAD_DIM
    blk_row = lax.broadcasted_iota(jnp.int32, (LANES, tm), 0)
    key_blk = chunk * (tm // SEL_BLOCK) + lax.broadcasted_iota(jnp.int32, (LANES, tm), 1) // SEL_BLOCK
    expand = (blk_row == key_blk).astype(BF16)
    for col in range(ks_ref.shape[1] // LANES):
        sl = slice(col * LANES, (col + 1) * LANES)
        ks_t = _rope(ks_ref[:, sl], tc_ref[...], tu_ref[...], td_ref[...]).T
        kw_t = _rope(kw_ref[:, sl], tc_ref[...], tu_ref[...], td_ref[...]).T
        vs, vw = vs_ref[:, sl], vw_ref[:, sl]
        for half in range(2):
            h = 2 * col + half
            if half == 1:
                ks_t = pltpu.roll(ks_t, NSA_HEAD_DIM, axis=0)
                kw_t = pltpu.roll(kw_t, NSA_HEAD_DIM, axis=0)
                vs = pltpu.roll(vs, NSA_HEAD_DIM, axis=1)
                vw = pltpu.roll(vw, NSA_HEAD_DIM, axis=1)
            ksa_ref[h, 0:LANES, :] = expand
            ksa_ref[h, LANES:2 * LANES, :] = jnp.where(row_lo, ks_t, 0.0).astype(BF16)
            kw_pad = jnp.where(row_lo, kw_t, 0.0).astype(BF16)
            for u in range(tm // Q_BLOCK):
                kwt_ref[h, u] = kw_pad[:, u * Q_BLOCK:(u + 1) * Q_BLOCK]
            vsa_ref[h] = jnp.where(lane_lo, vs, ones_col).astype(BF16)
            vwa_ref[h] = jnp.where(lane_lo, vw, ones_col).astype(BF16)


def _kv_post(ks, vs, kw, vw, tables):
    bsz, s_len, dk = ks.shape
    tm = KEY_CHUNK
    n_ch = s_len // tm
    upc = tm // Q_BLOCK
    inp = pl.BlockSpec((None, tm, dk), lambda b, c: (b, c, 0))
    tab = pl.BlockSpec((tm, LANES), lambda b, c: (c, 0))
    hk = NSA_KV_HEADS
    outs = pl.pallas_call(
        _kv_post_kernel,
        grid=(bsz, n_ch),
        in_specs=[inp, inp, inp, inp, tab, tab, tab],
        out_specs=[pl.BlockSpec((None, hk, None, 2 * LANES, tm), lambda b, c: (b, 0, c, 0, 0)),
                   pl.BlockSpec((None, hk, None, tm, LANES), lambda b, c: (b, 0, c, 0, 0)),
                   pl.BlockSpec((None, hk, upc, LANES, Q_BLOCK), lambda b, c: (b, 0, c, 0, 0)),
                   pl.BlockSpec((None, hk, None, tm, LANES), lambda b, c: (b, 0, c, 0, 0))],
        out_shape=[jax.ShapeDtypeStruct((bsz, hk, n_ch, 2 * LANES, tm), BF16),
                   jax.ShapeDtypeStruct((bsz, hk, n_ch, tm, LANES), BF16),
                   jax.ShapeDtypeStruct((bsz, hk, n_ch * upc, LANES, Q_BLOCK), BF16),
                   jax.ShapeDtypeStruct((bsz, hk, n_ch, tm, LANES), BF16)],
        compiler_params=_params("parallel", "parallel"),
        name="kv_post",
    )(ks, vs, kw, vw, *tables)
    ks_aug, vs_aug, kw_t, vw_aug = outs
    return ks_aug, vs_aug, kw_t, vw_aug.reshape(bsz, hk, n_ch * upc, Q_BLOCK, LANES)


def _compress_kernel(x_ref, pe_ref, w1_ref, w2_ref, o_ref, *, transpose_out):
    half = w1_ref.shape[0] // 2
    xb = x_ref[...].astype(BF16)
    y0 = jnp.dot(xb, w1_ref[0:half, :], preferred_element_type=F32)
    y1 = jnp.dot(xb, w1_ref[half:2 * half, :], preferred_element_type=F32)
    bias = jnp.dot(pe_ref[...].astype(BF16), w1_ref[...], preferred_element_type=F32)[0:1, :]
    n = y0.shape[0]
    hid = y0 + pltpu.roll(y1, n - 1, axis=0) + bias
    out = jnp.dot(_silu(hid).astype(BF16), w2_ref[...], preferred_element_type=F32)
    if transpose_out:
        o_ref[...] = out.T.astype(BF16)
    else:
        lane = lax.broadcasted_iota(jnp.int32, (1, LANES), 1)
        o_ref[...] = jnp.where(lane < NSA_HEAD_DIM, out, (lane == NSA_HEAD_DIM).astype(F32)).astype(BF16)


def _compress(t, pe, w1, w2, transpose_out):
    bsz, s_len, _ = t.shape
    hk, dk = NSA_KV_HEADS, NSA_HEAD_DIM
    n = s_len // CMP_STRIDE
    xr = t.reshape(bsz, n, CMP_STRIDE, hk, dk).transpose(0, 3, 1, 2, 4).reshape(bsz, hk, n, CMP_STRIDE * dk)
    pe8 = jnp.broadcast_to(pe.reshape(1, CMP_BLOCK * dk), (8, CMP_BLOCK * dk))
    w2p = jnp.pad(w2, ((0, 0), (0, LANES - dk))).astype(BF16)
    o_shape = (bsz, hk, LANES, n) if transpose_out else (bsz, hk, n, LANES)
    const = lambda b, h: (0, 0)
    return pl.pallas_call(
        functools.partial(_compress_kernel, transpose_out=transpose_out),
        grid=(bsz, hk),
        in_specs=[pl.BlockSpec((None, None, n, CMP_STRIDE * dk), lambda b, h: (b, h, 0, 0)),
                  pl.BlockSpec(pe8.shape, const),
                  pl.BlockSpec(w1.shape, const),
                  pl.BlockSpec(w2p.shape, const)],
        out_specs=pl.BlockSpec((None, None) + o_shape[2:], lambda b, h: (b, h, 0, 0)),
        out_shape=jax.ShapeDtypeStruct(o_shape, BF16),
        compiler_params=_params("parallel", "parallel"),
        name="compress",
    )(xr, pe8, w1.astype(BF16), w2p)


def _importance_matrix(n_cmp_rows, n_blocks):
    c = np.arange(n_cmp_rows)[:, None]
    j = np.arange(n_blocks)[None, :]
    per = SEL_BLOCK // CMP_STRIDE
    a = ((c >= per * j) & (c <= per * j + per - 1)).astype(np.float32)
    for r in range(1, CMP_BLOCK // CMP_STRIDE):
        a += ((c + r >= per * j) & (c + r <= per * j + per - 1)).astype(np.float32)
    return a


def _dot_split3(a, w):
    out = None
    for _ in range(3):
        part = a.astype(BF16)
        a = a - part.astype(F32)
        term = jnp.dot(part, w, preferred_element_type=F32)
        out = term if out is None else out + term
    return out


_SORT16 = ((0, 1), (2, 3), (0, 2), (1, 3), (1, 2), (4, 5), (6, 7), (4, 6), (5, 7), (5, 6), (0, 4), (2, 6), (2, 4),
           (1, 5), (3, 7), (3, 5), (1, 2), (3, 4), (5, 6), (8, 9), (10, 11), (8, 10), (9, 11), (9, 10), (12, 13),
           (14, 15), (12, 14), (13, 15), (13, 14), (8, 12), (10, 14), (10, 12), (9, 13), (11, 15), (11, 13), (9, 10),
           (11, 12), (13, 14), (0, 8), (4, 12), (4, 8), (2, 10), (6, 14), (6, 10), (2, 4), (6, 8), (10, 12), (1, 9),
           (5, 13), (5, 9), (3, 11), (7, 15), (7, 11), (3, 5), (7, 9), (11, 13), (1, 2), (3, 4), (5, 6), (7, 8),
           (9, 10), (11, 12), (13, 14))
_BITONIC16 = ((0, 8), (1, 9), (2, 10), (3, 11), (4, 12), (5, 13), (6, 14), (7, 15), (0, 4), (1, 5), (2, 6), (3, 7),
              (0, 2), (1, 3), (0, 1), (2, 3), (4, 6), (5, 7), (4, 5), (6, 7), (8, 12), (9, 13), (10, 14), (11, 15),
              (8, 10), (9, 11), (8, 9), (10, 11), (12, 14), (13, 15), (12, 13), (14, 15))


def _top_selection_bias(imp, blk0):
    nb = imp.shape[1]
    tl = lax.broadcasted_iota(jnp.int32, (Q_BLOCK, nb), 0)
    jb = lax.broadcasted_iota(jnp.int32, (Q_BLOCK, nb), 1)
    qblk = blk0 + tl // SEL_BLOCK
    forced = (jb == 0) | (jb == qblk) | (jb == qblk - 1)
    score = jnp.where(forced, jnp.inf, jnp.where(jb <= qblk, imp, -1.0))
    sc_t = score.T

    assert nb == N_SEL * 8
    v = [sc_t[8 * k:8 * k + 8] for k in range(N_SEL)]

    def exchange(net):
        for a_, b_ in net:
            hi, lo = jnp.maximum(v[a_], v[b_]), jnp.minimum(v[a_], v[b_])
            v[a_], v[b_] = hi, lo

    exchange(_SORT16)
    for shift in (4, 2):
        w = [pltpu.roll(x, shift, axis=0) for x in v]
        v = [jnp.maximum(v[k], w[N_SEL - 1 - k]) for k in range(N_SEL)]
        exchange(_BITONIC16)
    w = [pltpu.roll(x, 1, axis=0) for x in v]
    tau8 = jnp.maximum(v[0], w[N_SEL - 1])
    for k in range(1, N_SEL):
        tau8 = jnp.minimum(tau8, jnp.maximum(v[k], w[N_SEL - 1 - k]))
    tau = tau8[0:1]
    above = sc_t > tau
    tied = sc_t == tau
    room = N_SEL - jnp.sum(above.astype(F32), axis=0, keepdims=True)
    blk_r = lax.broadcasted_iota(jnp.int32, (nb, nb), 0)
    blk_c = lax.broadcasted_iota(jnp.int32, (nb, nb), 1)
    ties_before = jnp.dot((blk_c < blk_r).astype(BF16), tied.astype(BF16), preferred_element_type=F32)
    chosen = above | (tied & (ties_before < room))
    blk_t = lax.broadcasted_iota(jnp.int32, sc_t.shape, 0)
    qblk_t = blk0 + lax.broadcasted_iota(jnp.int32, sc_t.shape, 1) // SEL_BLOCK
    return jnp.where(chosen & (blk_t <= qblk_t), 0.0, -MASK_BIAS).T.astype(BF16)


def _nsa_kernel(qc_ref, qr_ref, gt_ref, kct_ref, vca_ref, ksa_ref, vsa_ref, kwt_ref, vwa_ref, imp_ref, dmask_ref,
                o_ref, sc_ref, part_ref, gsel_ref, *, n_qb):
    i0 = pl.program_id(2) * n_qb
    rows_qb = NSA_GROUP * Q_BLOCK
    rows = n_qb * rows_qb
    n_cmp = kct_ref.shape[1]
    qc = qc_ref[...].reshape(rows, LANES)
    qr = qr_ref[...].reshape(rows, LANES)
    row = lax.broadcasted_iota(jnp.int32, (rows, 1), 0)
    tok = row % Q_BLOCK
    t = (i0 + row // rows_qb) * Q_BLOCK + tok

    s = jnp.dot(qc, kct_ref[...], preferred_element_type=F32)
    cmp_end = lax.broadcasted_iota(jnp.int32, (1, n_cmp), 1) * CMP_STRIDE + (CMP_BLOCK - 1)
    s = jnp.where(cmp_end <= t, s, -jnp.inf)
    m = jnp.max(s, axis=-1, keepdims=True)
    m = jnp.where(m == -jnp.inf, 0.0, m)
    p = jnp.exp2(s - m)
    d = jnp.sum(p, axis=-1, keepdims=True)
    p = p * (1.0 / jnp.where(d > 0, d, 1.0))
    o_cmp = jnp.dot(p.astype(BF16), vca_ref[...], preferred_element_type=F32)

    cw = lax.broadcasted_iota(jnp.int32, (1, WIN_UNITS * Q_BLOCK), 1)
    tok_qb = tok[0:rows_qb]
    o_win = []
    for qb in range(n_qb):
        iq = i0 + qb
        units = [jnp.maximum(iq - (WIN_UNITS - 1) + u, 0) for u in range(WIN_UNITS)]
        kw = jnp.concatenate([kwt_ref[u] for u in units], axis=1)
        vw = jnp.concatenate([vwa_ref[u] for u in units], axis=0)
        sw = jnp.dot(qr[qb * rows_qb:(qb + 1) * rows_qb], kw, preferred_element_type=F32)
        wmask = (cw <= tok_qb + WINDOW) & (cw > tok_qb) & (cw // Q_BLOCK + iq >= WIN_UNITS - 1)
        sw = jnp.where(wmask, sw, -jnp.inf)
        pw = jnp.exp2(sw.astype(BF16) - jnp.max(sw, axis=-1, keepdims=True).astype(BF16))
        acc_w = jnp.dot(pw, vw, preferred_element_type=F32)
        o_win.append(acc_w * (1.0 / acc_w[:, NSA_HEAD_DIM:NSA_HEAD_DIM + 1]))
    o_win = jnp.concatenate(o_win, axis=0)

    gates = jax.nn.sigmoid(gt_ref[...])
    for qb in range(n_qb):
        gq = gates[qb * Q_BLOCK:(qb + 1) * Q_BLOCK]
        for g in range(NSA_GROUP):
            rs = slice(qb * rows_qb + g * Q_BLOCK, qb * rows_qb + (g + 1) * Q_BLOCK)
            c0 = g * N_BRANCH
            part_ref[rs, :] = gq[:, c0:c0 + 1] * o_cmp[rs] + gq[:, c0 + 2:c0 + 3] * o_win[rs]
            gsel_ref[rs, :] = jnp.broadcast_to(gq[:, c0 + 1:c0 + 2], (Q_BLOCK, LANES))

    biases = []
    for qb in range(n_qb):
        base = qb * rows_qb
        p_grp = p[base:base + Q_BLOCK]
        for g in range(1, NSA_GROUP):
            p_grp = p_grp + p[base + g * Q_BLOCK:base + (g + 1) * Q_BLOCK]
        imp = _dot_split3(p_grp, imp_ref[...])
        bias = _top_selection_bias(imp, (i0 + qb) * (Q_BLOCK // SEL_BLOCK))
        biases += [bias] * NSA_GROUP
    q_aug = jnp.concatenate([jnp.concatenate(biases, axis=0), qr], axis=1)

    def scores(ck, slot):
        sc_ref[slot] = jnp.dot(q_aug, ksa_ref[ck], preferred_element_type=F32).astype(BF16)

    def absorb(sc, ck, m_run, acc):
        m_new = jnp.maximum(m_run, jnp.max(sc, axis=-1, keepdims=True).astype(F32))
        pp = jnp.exp2(sc - m_new.astype(BF16))
        acc = jnp.exp2(m_run - m_new) * acc + jnp.dot(pp, vsa_ref[ck], preferred_element_type=F32)
        return m_new, acc

    last = (i0 * Q_BLOCK) // KEY_CHUNK

    def pair_body(kk, carry):
        scores(2 * kk + 1, 1)
        carry = absorb(sc_ref[0], 2 * kk, *carry)
        scores(2 * kk + 2, 0)
        return absorb(sc_ref[1], 2 * kk + 1, *carry)

    def odd_body(_, carry):
        scores(last, 1)
        return absorb(sc_ref[0], last - 1, *carry)

    scores(0, 0)
    carry = (jnp.full((rows, 1), -jnp.inf, F32), jnp.zeros((rows, LANES), F32))
    carry = lax.fori_loop(0, last // 2, pair_body, carry)
    carry = lax.fori_loop(0, last % 2, odd_body, carry)
    _, acc_s = absorb(sc_ref[last % 2] + dmask_ref[...], last, *carry)
    o_sel = acc_s * (1.0 / acc_s[:, NSA_HEAD_DIM:NSA_HEAD_DIM + 1])

    o = part_ref[...] + gsel_ref[...] * o_sel
    lane_lo = lax.broadcasted_iota(jnp.int32, (1, LANES), 1) < NSA_HEAD_DIM
    for qb in range(n_qb):
        heads = [o[qb * rows_qb + g * Q_BLOCK:qb * rows_qb + (g + 1) * Q_BLOCK] for g in range(NSA_GROUP)]
        pairs = [jnp.where(lane_lo, heads[2 * k], pltpu.roll(heads[2 * k + 1], NSA_HEAD_DIM, axis=1))
                 for k in range(NSA_GROUP // 2)]
        o_ref[qb * Q_BLOCK:(qb + 1) * Q_BLOCK, :] = jnp.concatenate(pairs, axis=1).astype(o_ref.dtype)


def _nsa_attention(qc, qr, gates_raw, kc_t, vc_aug, ks_aug, vs_aug, kw_t, vw_aug, imp_mat, n_qb=4):
    bsz, nq, hk, rows_qb, _ = qc.shape
    s_len = nq * Q_BLOCK
    n_ch = ks_aug.shape[2]
    n_units = kw_t.shape[2]
    n_cmp = kc_t.shape[3]
    assert nq % n_qb == 0 and KEY_CHUNK == n_qb * Q_BLOCK
    r = np.arange(n_qb * rows_qb)
    own = (r // rows_qb) * Q_BLOCK + r % Q_BLOCK
    dmask = jnp.asarray(np.where(np.arange(KEY_CHUNK)[None, :] <= own[:, None], 0.0, -np.inf), dtype=BF16)
    qspec = pl.BlockSpec((None, n_qb, None, rows_qb, LANES), lambda b, h, i: (b, i, h, 0, 0))
    per_head = lambda shp: pl.BlockSpec((None, None) + shp, lambda b, h, i: (b, h) + (0,) * len(shp))
    return pl.pallas_call(
        functools.partial(_nsa_kernel, n_qb=n_qb),
        grid=(bsz, hk, nq // n_qb),
        in_specs=[qspec, qspec,
                  pl.BlockSpec((None, n_qb * Q_BLOCK, LANES), lambda b, h, i: (b, i, h)),
                  per_head((LANES, n_cmp)),
                  per_head((n_cmp, LANES)),
                  per_head((n_ch, 2 * LANES, KEY_CHUNK)),
                  per_head((n_ch, KEY_CHUNK, LANES)),
                  per_head((n_units, LANES, Q_BLOCK)),
                  per_head((n_units, Q_BLOCK, LANES)),
                  pl.BlockSpec(imp_mat.shape, lambda b, h, i: (0, 0)),
                  pl.BlockSpec(dmask.shape, lambda b, h, i: (0, 0))],
        out_specs=pl.BlockSpec((None, n_qb * Q_BLOCK, NSA_GROUP * NSA_HEAD_DIM), lambda b, h, i: (b, i, h)),
        out_shape=jax.ShapeDtypeStruct((bsz, s_len, hk * NSA_GROUP * NSA_HEAD_DIM), BF16),
        scratch_shapes=[pltpu.VMEM((2, n_qb * rows_qb, KEY_CHUNK), BF16),
                        pltpu.VMEM((n_qb * rows_qb, LANES), F32),
                        pltpu.VMEM((n_qb * rows_qb, LANES), F32)],
        compiler_params=_params("parallel", "parallel", "arbitrary"),
        name="nsa_attention",
    )(qc, qr, gates_raw, kc_t, vc_aug, ks_aug, vs_aug, kw_t, vw_aug, imp_mat, dmask)


def kernel(x, c, mod_w, mod_b, norm_pre_mix, norm_post_mix, norm_pre_ffn, norm_post_ffn, ffn_w_in, ffn_w_out, ssm_w_in, ssm_conv_w, ssm_conv_b, ssm_dt_bias, ssm_a_log, ssm_d, ssm_norm, ssm_w_out, kv_norm, kv_mod_w, kv_mod_b, w_kv, cmp_pos_k, cmp_w1_k, cmp_w2_k, cmp_pos_v, cmp_w1_v, cmp_w2_v, nsa_w_q, nsa_w_o):
    bsz, s_len, d = x.shape
    depth = mod_w.shape[0]
    n_a = ssm_w_in.shape[0]
    d_ff = ffn_w_out.shape[1]
    d_inner = ssm_w_out.shape[1]
    n_ssm_heads = d_inner // SSM_HEADDIM
    d_xbc = ssm_conv_w.shape[-1]
    dq = NSA_HEADS * NSA_HEAD_DIM
    dkv = NSA_KV_HEADS * NSA_HEAD_DIM
    assert s_len % KEY_CHUNK == 0 and s_len // SEL_BLOCK <= LANES and s_len // SEL_BLOCK >= N_SEL

    mods = _mods(c, mod_w, mod_b)
    kv_mods = _mods(c, kv_mod_w[None], kv_mod_b[None])[0]
    vec = lambda m, k: m[:, k * d:(k + 1) * d].reshape(bsz, 1, d)

    shared = None
    tables = None
    for i in range(depth):
        sh_m, sc_m, g_m, sh_f, sc_f, g_f = [vec(mods[i], k) for k in range(6)]
        if i == n_a:
            tables = _rope_tables(s_len)
            w_kv_b = w_kv.astype(BF16)
            kc, vc, ks, vs, kw, vw = _adaln_mm(x, kv_norm, vec(kv_mods, 0), vec(kv_mods, 1),
                                               [w_kv_b[:, k * dkv:(k + 1) * dkv] for k in range(6)])
            kc_t = _compress(kc, cmp_pos_k, cmp_w1_k, cmp_w2_k, transpose_out=True)
            vc_aug = _compress(vc, cmp_pos_v, cmp_w1_v, cmp_w2_v, transpose_out=False)
            shared = (kc_t, vc_aug) + _kv_post(ks, vs, kw, vw, tables)
            imp_mat = jnp.asarray(_importance_matrix(s_len // CMP_STRIDE, LANES), dtype=BF16)
        if i < n_a:
            w_in = ssm_w_in[i].astype(BF16)
            w_dt = jnp.pad(w_in[:, d_inner + d_xbc:], ((0, 0), (0, LANES - n_ssm_heads)))
            z, xbc, dt_raw = _adaln_mm(x, norm_pre_mix[i], sh_m, sc_m,
                                       [w_in[:, :d_inner], w_in[:, d_inner:d_inner + d_xbc], w_dt])
            y = _ssd(xbc, z, dt_raw, ssm_conv_w[i], ssm_conv_b[i], ssm_dt_bias[i], ssm_a_log[i],
                     ssm_d[i], ssm_norm[i])
            x = _mm_post(y, ssm_w_out[i].astype(BF16), x, g_m, norm_post_mix[i])
        else:
            w_q = nsa_w_q[i - n_a].astype(BF16)
            w_g = w_q[:, dq:].reshape(d, NSA_KV_HEADS, NSA_GROUP * N_BRANCH)
            w_g = jnp.pad(w_g, ((0, 0), (0, 0), (0, LANES - NSA_GROUP * N_BRANCH))).reshape(d, NSA_KV_HEADS * LANES)
            q, gates_raw = _adaln_mm(x, norm_pre_mix[i], sh_m, sc_m, [w_q[:, :dq], w_g])
            qc, qr = _q_post(q, tables)
            o = _nsa_attention(qc, qr, gates_raw, *shared, imp_mat)
            x = _mm_post(o, nsa_w_o[i - n_a].astype(BF16), x, g_m, norm_post_mix[i])
        w_ffn = ffn_w_in[i].astype(BF16)
        x = _ffn(x, norm_pre_ffn[i], sh_f, sc_f, w_ffn[:, :d_ff], w_ffn[:, d_ff:], ffn_w_out[i].astype(BF16),
                 norm_post_ffn[i], g_f)
    return x
```

```python
import functools
import math

import numpy as np
import jax
import jax.numpy as jnp
from jax import lax
from jax.experimental import pallas as pl
from jax.experimental.pallas import tpu as pltpu

F32 = jnp.float32
BF16 = jnp.bfloat16

EPS = 1e-6
DEPTH = 4
N_A = DEPTH // 2

SSM_HEADDIM = 64
SSM_STATE = 128
SSM_GROUPS = 4
CONV_W = 4
SSD_CHUNK = 256

NSA_HEADS = 16
NSA_KV_HEADS = 4
NSA_GROUP = NSA_HEADS // NSA_KV_HEADS
NSA_HEAD_DIM = 64
CMP_BLOCK = 32
CMP_STRIDE = 16
SEL_BLOCK = 64
N_SEL = 16
WINDOW = 512
Q_BLOCK = 128
N_BRANCH = 3
ROPE_THETA = 500000.0
ROT_DIMS = NSA_HEAD_DIM // 4

LANES = 128
KEY_CHUNK = 512
WIN_UNITS = WINDOW // Q_BLOCK + 1
MASK_BIAS = float(2 ** 30)
VMEM_LIMIT = 56 * 1024 * 1024


def _params(*sem):
    return pltpu.CompilerParams(dimension_semantics=sem, vmem_limit_bytes=VMEM_LIMIT)


def _rms(x, g):
    return x * lax.rsqrt(jnp.mean(x * x, axis=-1, keepdims=True) + EPS) * g


def _silu(x):
    return x * jax.nn.sigmoid(x)


def _mods_kernel(c_ref, w_ref, b_ref, o_ref):
    cs = _silu(c_ref[...])
    o_ref[...] = jnp.dot(cs, w_ref[...], preferred_element_type=F32) + b_ref[...]


def _mods(c, w, b):
    n_l, d, n = w.shape
    bsz = c.shape[0]
    tn = 1024
    return pl.pallas_call(
        _mods_kernel,
        grid=(n_l, n // tn),
        in_specs=[pl.BlockSpec((bsz, d), lambda l, j: (0, 0)),
                  pl.BlockSpec((None, d, tn), lambda l, j: (l, 0, j)),
                  pl.BlockSpec((None, 1, tn), lambda l, j: (l, 0, j))],
        out_specs=pl.BlockSpec((None, bsz, tn), lambda l, j: (l, 0, j)),
        out_shape=jax.ShapeDtypeStruct((n_l, bsz, n), F32),
        compiler_params=_params("parallel", "parallel"),
        name="mods",
    )(c, w, b.reshape(n_l, 1, n))


def _adaln_mm_kernel(x_ref, g_ref, sh_ref, sc_ref, *refs, n_w):
    h = _rms(x_ref[...], g_ref[...]) * (1.0 + sc_ref[...]) + sh_ref[...]
    hb = h.astype(BF16)
    for w_ref, o_ref in zip(refs[:n_w], refs[n_w:]):
        o_ref[...] = jnp.dot(hb, w_ref[...], preferred_element_type=F32).astype(o_ref.dtype)


def _adaln_mm(x, g, shift, scale, ws, tm=256):
    bsz, s_len, d = x.shape
    n_w = len(ws)
    const = lambda b, i: (0, 0)
    return pl.pallas_call(
        functools.partial(_adaln_mm_kernel, n_w=n_w),
        grid=(bsz, s_len // tm),
        in_specs=[pl.BlockSpec((None, tm, d), lambda b, i: (b, i, 0)),
                  pl.BlockSpec((1, d), const),
                  pl.BlockSpec((None, 1, d), lambda b, i: (b, 0, 0)),
                  pl.BlockSpec((None, 1, d), lambda b, i: (b, 0, 0))]
                 + [pl.BlockSpec(w.shape, const) for w in ws],
        out_specs=[pl.BlockSpec((None, tm, w.shape[1]), lambda b, i: (b, i, 0)) for w in ws],
        out_shape=[jax.ShapeDtypeStruct((bsz, s_len, w.shape[1]), F32) for w in ws],
        compiler_params=_params("parallel", "parallel"),
        name="adaln_mm",
    )(x, g.reshape(1, d), shift, scale, *ws)


def _mm_post_kernel(a_ref, w_ref, x_ref, gate_ref, g_ref, o_ref):
    y = jnp.dot(a_ref[...], w_ref[...], preferred_element_type=F32)
    o_ref[...] = x_ref[...] + gate_ref[...] * _rms(y, g_ref[...])


def _mm_post(a, w, x, gate, g, tm=512):
    bsz, s_len, d = x.shape
    k = a.shape[-1]
    return pl.pallas_call(
        _mm_post_kernel,
        grid=(bsz, s_len // tm),
        in_specs=[pl.BlockSpec((None, tm, k), lambda b, i: (b, i, 0)),
                  pl.BlockSpec((k, d), lambda b, i: (0, 0)),
                  pl.BlockSpec((None, tm, d), lambda b, i: (b, i, 0)),
                  pl.BlockSpec((None, 1, d), lambda b, i: (b, 0, 0)),
                  pl.BlockSpec((1, d), lambda b, i: (0, 0))],
        out_specs=pl.BlockSpec((None, tm, d), lambda b, i: (b, i, 0)),
        out_shape=jax.ShapeDtypeStruct(x.shape, F32),
        compiler_params=_params("parallel", "parallel"),
        name="mm_post",
    )(a, w, x, gate, g.reshape(1, d))


def _ffn_kernel(x_ref, g1_ref, sh_ref, sc_ref, wa_ref, wb_ref, wo_ref, g2_ref, gate_ref, o_ref, u_ref, *, tc):
    x = x_ref[...]
    hb = (_rms(x, g1_ref[...]) * (1.0 + sc_ref[...]) + sh_ref[...]).astype(BF16)
    d_ff = wa_ref.shape[1]
    for c0 in range(0, d_ff, tc):
        a = jnp.dot(hb, wa_ref[:, c0:c0 + tc], preferred_element_type=F32)
        b = jnp.dot(hb, wb_ref[:, c0:c0 + tc], preferred_element_type=F32)
        u_ref[:, c0:c0 + tc] = (_silu(a) * b).astype(BF16)
    y = jnp.dot(u_ref[...], wo_ref[...], preferred_element_type=F32)
    o_ref[...] = x + gate_ref[...] * _rms(y, g2_ref[...])


def _ffn(x, g1, shift, scale, wa, wb, wo, g2, gate, tm=512, tc=256):
    bsz, s_len, d = x.shape
    d_ff = wa.shape[1]
    const = lambda b, i: (0, 0)
    vec = lambda b, i: (b, 0, 0)
    return pl.pallas_call(
        functools.partial(_ffn_kernel, tc=tc),
        grid=(bsz, s_len // tm),
        in_specs=[pl.BlockSpec((None, tm, d), lambda b, i: (b, i, 0)),
                  pl.BlockSpec((1, d), const),
                  pl.BlockSpec((None, 1, d), vec),
                  pl.BlockSpec((None, 1, d), vec),
                  pl.BlockSpec((d, d_ff), const),
                  pl.BlockSpec((d, d_ff), const),
                  pl.BlockSpec((d_ff, d), const),
                  pl.BlockSpec((1, d), const),
                  pl.BlockSpec((None, 1, d), vec)],
        out_specs=pl.BlockSpec((None, tm, d), lambda b, i: (b, i, 0)),
        out_shape=jax.ShapeDtypeStruct(x.shape, F32),
        scratch_shapes=[pltpu.VMEM((tm, d_ff), BF16)],
        compiler_params=_params("parallel", "parallel"),
        name="ffn",
    )(x, g1.reshape(1, d), shift, scale, wa, wb, wo, g2.reshape(1, d), gate)


def _pair(mat, p, lane_lo):
    return jnp.where(lane_lo, mat[:, 2 * p:2 * p + 1], mat[:, 2 * p + 1:2 * p + 2])


def _ssd_kernel(xbc_ref, z_ref, dt_ref, cw_ref, cb_ref, dtb_ref, alog_ref, dsk_ref, ng_ref, o_ref,
                ext_ref, xs_ref, state_ref, y_ref, *, d_inner, n_groups):
    L = SSD_CHUNK
    N = SSM_STATE
    gw = d_inner // n_groups
    pairs_per_group = gw // LANES
    n_pairs = d_inner // LANES
    c_idx = pl.program_id(1)

    @pl.when(c_idx == 0)
    def _():
        ext_ref[0:8, :] = jnp.zeros((8, ext_ref.shape[1]), F32)
        state_ref[...] = jnp.zeros(state_ref.shape, F32)

    ext_ref[8:8 + L, :] = xbc_ref[...]
    d_xbc = ext_ref.shape[1]
    bc = []
    for c0 in range(0, d_xbc, 512):
        acc = cb_ref[:, c0:c0 + 512] + cw_ref[0:1, c0:c0 + 512] * ext_ref[5:5 + L, c0:c0 + 512]
        for k in range(1, CONV_W):
            acc = acc + cw_ref[k:k + 1, c0:c0 + 512] * ext_ref[5 + k:5 + k + L, c0:c0 + 512]
        acc = _silu(acc)
        if c0 < d_inner:
            xs_ref[:, c0:c0 + 512] = acc
        else:
            bc.append(acc)
    ext_ref[0:8, :] = ext_ref[L:L + 8, :]
    b_all = jnp.concatenate(bc[:len(bc) // 2], axis=1)
    c_all = jnp.concatenate(bc[len(bc) // 2:], axis=1)

    x_raw = dt_ref[...] + dtb_ref[...]
    dtv = jnp.maximum(x_raw, 0.0) + jnp.log1p(jnp.exp(-jnp.abs(x_raw)))
    da = dtv * (-jnp.exp(alog_ref[...]))
    ri = lax.broadcasted_iota(jnp.int32, (L, L), 0)
    ci = lax.broadcasted_iota(jnp.int32, (L, L), 1)
    causal = ci <= ri
    acs = jnp.dot(causal.astype(F32), da, preferred_element_type=F32, precision=lax.Precision.HIGHEST)
    acs_t = acs.T
    acs_last = acs[L - 1:L, :]
    e_acs = jnp.exp(acs)
    tail = jnp.exp(acs_last - acs)
    e_last = jnp.exp(acs_last)
    lane_lo = lax.broadcasted_iota(jnp.int32, (1, LANES), 1) < SSM_HEADDIM

    cb = None
    for p in range(n_pairs):
        g = p // pairs_per_group
        if p % pairs_per_group == 0:
            bg = b_all[:, g * N:(g + 1) * N]
            cg = c_all[:, g * N:(g + 1) * N].astype(BF16)
            cb = lax.dot_general(cg, bg.astype(BF16), (((1,), (1,)), ((), ())), preferred_element_type=F32)
            bg_t = bg.T.astype(BF16)
        x2 = xs_ref[:, p * LANES:(p + 1) * LANES]
        xdt = x2 * _pair(dtv, p, lane_lo)
        y2 = None
        for half in range(2):
            h = 2 * p + half
            seg = acs[:, h:h + 1] - acs_t[h:h + 1, :]
            m = (cb * jnp.exp(jnp.where(causal, seg, -jnp.inf))).astype(BF16)
            keep = lane_lo if half == 0 else jnp.logical_not(lane_lo)
            part = jnp.dot(m, jnp.where(keep, xdt, 0.0).astype(BF16), preferred_element_type=F32)
            y2 = part if y2 is None else y2 + part
        st = state_ref[:, p * LANES:(p + 1) * LANES]
        y2 = y2 + jnp.dot(cg, st.astype(BF16), preferred_element_type=F32) * _pair(e_acs, p, lane_lo)
        y2 = y2 + dsk_ref[:, p * LANES:(p + 1) * LANES] * x2
        y_ref[:, p * LANES:(p + 1) * LANES] = y2
        w2 = (xdt * _pair(tail, p, lane_lo)).astype(BF16)
        state_ref[:, p * LANES:(p + 1) * LANES] = (
            st * _pair(e_last, p, lane_lo) + jnp.dot(bg_t, w2, preferred_element_type=F32))

    for g in range(n_groups):
        sl = slice(g * gw, (g + 1) * gw)
        yg = y_ref[:, sl] * _silu(z_ref[:, sl])
        o_ref[:, sl] = _rms(yg, ng_ref[:, sl]).astype(o_ref.dtype)


def _ssd(xbc, z, dt_raw, conv_w, conv_b, dt_bias, a_log, d_skip, norm_g):
    bsz, s_len, d_xbc = xbc.shape
    d_inner = z.shape[-1]
    n_heads = d_inner // SSM_HEADDIM
    L = SSD_CHUNK
    pad = LANES - n_heads
    row = lambda v: jnp.pad(v.astype(F32), (0, pad)).reshape(1, LANES)
    const = lambda b, c: (0, 0)
    blk = lambda w: pl.BlockSpec((None, L, w), lambda b, c: (b, c, 0))
    return pl.pallas_call(
        functools.partial(_ssd_kernel, d_inner=d_inner, n_groups=SSM_GROUPS),
        grid=(bsz, s_len // L),
        in_specs=[blk(d_xbc), blk(d_inner), blk(LANES),
                  pl.BlockSpec((CONV_W, d_xbc), const),
                  pl.BlockSpec((1, d_xbc), const),
                  pl.BlockSpec((1, LANES), const),
                  pl.BlockSpec((1, LANES), const),
                  pl.BlockSpec((1, d_inner), const),
                  pl.BlockSpec((1, d_inner), const)],
        out_specs=blk(d_inner),
        out_shape=jax.ShapeDtypeStruct((bsz, s_len, d_inner), BF16),
        scratch_shapes=[pltpu.VMEM((L + 8, d_xbc), F32),
                        pltpu.VMEM((L, d_inner), F32),
                        pltpu.VMEM((SSM_STATE, d_inner), F32),
                        pltpu.VMEM((L, d_inner), F32)],
        compiler_params=_params("parallel", "arbitrary"),
        name="ssd",
    )(xbc, z, dt_raw, conv_w, conv_b.reshape(1, d_xbc), row(dt_bias), row(a_log),
      jnp.repeat(d_skip.astype(F32), SSM_HEADDIM).reshape(1, d_inner), norm_g.reshape(1, d_inner))


def _rope_tables(s_len):
    half = ROT_DIMS // 2
    inv_freq = ROPE_THETA ** (-jnp.arange(half, dtype=F32) / half)
    ang = jnp.arange(s_len).astype(F32)[:, None] * inv_freq[None, :]
    cos, sin = jnp.cos(ang), jnp.sin(ang)
    rest = NSA_HEAD_DIM - ROT_DIMS
    one = jnp.ones((s_len, rest), F32)
    zero = jnp.zeros((s_len, rest), F32)
    zh = jnp.zeros((s_len, half), F32)
    t_cos = jnp.concatenate([cos, cos, one], axis=1)
    t_up = jnp.concatenate([-sin, zh, zero], axis=1)
    t_dn = jnp.concatenate([zh, sin, zero], axis=1)
    two = lambda t: jnp.concatenate([t, t], axis=1)
    return two(t_cos), two(t_up), two(t_dn)


def _rope(x, t_cos, t_up, t_dn):
    half = ROT_DIMS // 2
    return x * t_cos + pltpu.roll(x, LANES - half, axis=1) * t_up + pltpu.roll(x, half, axis=1) * t_dn


def _q_proj_kernel(x_ref, g_ref, sh_ref, sc_ref, wq_ref, wg_ref, tc_ref, tu_ref, td_ref, qc_ref, qr_ref, gt_ref):
    h = _rms(x_ref[...], g_ref[...]) * (1.0 + sc_ref[...]) + sh_ref[...]
    hb = h.astype(BF16)
    gt_ref[...] = jnp.dot(hb, wg_ref[...], preferred_element_type=F32)
    q = jnp.dot(hb, wq_ref[...], preferred_element_type=F32)
    scale = NSA_HEAD_DIM ** -0.5 * math.log2(math.e)
    lane_lo = lax.broadcasted_iota(jnp.int32, (1, LANES), 1) < NSA_HEAD_DIM
    for blk in range(qc_ref.shape[0]):
        rs = slice(blk * Q_BLOCK, (blk + 1) * Q_BLOCK)
        for col in range(q.shape[1] // LANES):
            x = q[rs, col * LANES:(col + 1) * LANES] * scale
            xr = _rope(x, tc_ref[rs, :], tu_ref[rs, :], td_ref[rs, :])
            for half in range(2):
                head = 2 * col + half
                kvh, g = head // NSA_GROUP, head % NSA_GROUP
                for src, dst in ((x, qc_ref), (xr, qr_ref)):
                    v = src if half == 0 else pltpu.roll(src, NSA_HEAD_DIM, axis=1)
                    dst[blk, kvh, g * Q_BLOCK:(g + 1) * Q_BLOCK, :] = jnp.where(lane_lo, v, 0.0).astype(BF16)


def _q_proj(x, g, shift, scale, w_q, w_g, tables, n_blk=4):
    bsz, s_len, d = x.shape
    nq = s_len // Q_BLOCK
    tm = n_blk * Q_BLOCK
    const = lambda b, i: (0, 0)
    vec = lambda b, i: (b, 0, 0)
    tab = pl.BlockSpec((tm, LANES), lambda b, i: (i, 0))
    out = pl.BlockSpec((None, n_blk, NSA_KV_HEADS, NSA_GROUP * Q_BLOCK, LANES), lambda b, i: (b, i, 0, 0, 0))
    shape = jax.ShapeDtypeStruct((bsz, nq, NSA_KV_HEADS, NSA_GROUP * Q_BLOCK, LANES), BF16)
    return pl.pallas_call(
        _q_proj_kernel,
        grid=(bsz, nq // n_blk),
        in_specs=[pl.BlockSpec((None, tm, d), lambda b, i: (b, i, 0)),
                  pl.BlockSpec((1, d), const),
                  pl.BlockSpec((None, 1, d), vec),
                  pl.BlockSpec((None, 1, d), vec),
                  pl.BlockSpec(w_q.shape, const),
                  pl.BlockSpec(w_g.shape, const),
                  tab, tab, tab],
        out_specs=[out, out, pl.BlockSpec((None, tm, w_g.shape[1]), lambda b, i: (b, i, 0))],
        out_shape=[shape, shape, jax.ShapeDtypeStruct((bsz, s_len, w_g.shape[1]), F32)],
        compiler_params=_params("parallel", "parallel"),
        name="q_proj",
    )(x, g.reshape(1, d), shift, scale, w_q, w_g, *tables)


def _kv_post_kernel(ks_ref, vs_ref, kw_ref, vw_ref, tc_ref, tu_ref, td_ref,
                    ksa_ref, vsa_ref, kwt_ref, vwa_ref):
    tm = ks_ref.shape[0]
    chunk = pl.program_id(1)
    lane = lax.broadcasted_iota(jnp.int32, (1, LANES), 1)
    lane_lo = lane < NSA_HEAD_DIM
    ones_col = (lane == NSA_HEAD_DIM).astype(F32)
    row_lo = lax.broadcasted_iota(jnp.int32, (LANES, 1), 0) < NSA_HEAD_DIM
    blk_row = lax.broadcasted_iota(jnp.int32, (LANES, tm), 0)
    key_blk = chunk * (tm // SEL_BLOCK) + lax.broadcasted_iota(jnp.int32, (LANES, tm), 1) // SEL_BLOCK
    expand = (blk_row == key_blk).astype(BF16)
    for col in range(ks_ref.shape[1] // LANES):
        sl = slice(col * LANES, (col + 1) * LANES)
        ks_t = _rope(ks_ref[:, sl], tc_ref[...], tu_ref[...], td_ref[...]).T
        kw_t = _rope(kw_ref[:, sl], tc_ref[...], tu_ref[...], td_ref[...]).T
        vs, vw = vs_ref[:, sl], vw_ref[:, sl]
        for half in range(2):
            h = 2 * col + half
            if half == 1:
                ks_t = pltpu.roll(ks_t, NSA_HEAD_DIM, axis=0)
                kw_t = pltpu.roll(kw_t, NSA_HEAD_DIM, axis=0)
                vs = pltpu.roll(vs, NSA_HEAD_DIM, axis=1)
                vw = pltpu.roll(vw, NSA_HEAD_DIM, axis=1)
            ksa_ref[h, 0:LANES, :] = expand
            ksa_ref[h, LANES:2 * LANES, :] = jnp.where(row_lo, ks_t, 0.0).astype(BF16)
            kw_pad = jnp.where(row_lo, kw_t, 0.0).astype(BF16)
            for u in range(tm // Q_BLOCK):
                kwt_ref[h, u] = kw_pad[:, u * Q_BLOCK:(u + 1) * Q_BLOCK]
            vsa_ref[h] = jnp.where(lane_lo, vs, ones_col).astype(BF16)
            vwa_ref[h] = jnp.where(lane_lo, vw, ones_col).astype(BF16)


def _kv_post(ks, vs, kw, vw, tables):
    bsz, s_len, dk = ks.shape
    tm = KEY_CHUNK
    n_ch = s_len // tm
    upc = tm // Q_BLOCK
    inp = pl.BlockSpec((None, tm, dk), lambda b, c: (b, c, 0))
    tab = pl.BlockSpec((tm, LANES), lambda b, c: (c, 0))
    hk = NSA_KV_HEADS
    outs = pl.pallas_call(
        _kv_post_kernel,
        grid=(bsz, n_ch),
        in_specs=[inp, inp, inp, inp, tab, tab, tab],
        out_specs=[pl.BlockSpec((None, hk, None, 2 * LANES, tm), lambda b, c: (b, 0, c, 0, 0)),
                   pl.BlockSpec((None, hk, None, tm, LANES), lambda b, c: (b, 0, c, 0, 0)),
                   pl.BlockSpec((None, hk, upc, LANES, Q_BLOCK), lambda b, c: (b, 0, c, 0, 0)),
                   pl.BlockSpec((None, hk, None, tm, LANES), lambda b, c: (b, 0, c, 0, 0))],
        out_shape=[jax.ShapeDtypeStruct((bsz, hk, n_ch, 2 * LANES, tm), BF16),
                   jax.ShapeDtypeStruct((bsz, hk, n_ch, tm, LANES), BF16),
                   jax.ShapeDtypeStruct((bsz, hk, n_ch * upc, LANES, Q_BLOCK), BF16),
                   jax.ShapeDtypeStruct((bsz, hk, n_ch, tm, LANES), BF16)],
        compiler_params=_params("parallel", "parallel"),
        name="kv_post",
    )(ks, vs, kw, vw, *tables)
    ks_aug, vs_aug, kw_t, vw_aug = outs
    return ks_aug, vs_aug, kw_t, vw_aug.reshape(bsz, hk, n_ch * upc, Q_BLOCK, LANES)


def _compress_kernel(x_ref, pe_ref, w1_ref, w2_ref, o_ref, *, transpose_out):
    half = w1_ref.shape[0] // 2
    xb = x_ref[...].astype(BF16)
    y0 = jnp.dot(xb, w1_ref[0:half, :], preferred_element_type=F32)
    y1 = jnp.dot(xb, w1_ref[half:2 * half, :], preferred_element_type=F32)
    bias = jnp.dot(pe_ref[...].astype(BF16), w1_ref[...], preferred_element_type=F32)[0:1, :]
    n = y0.shape[0]
    hid = y0 + pltpu.roll(y1, n - 1, axis=0) + bias
    out = jnp.dot(_silu(hid).astype(BF16), w2_ref[...], preferred_element_type=F32)
    if transpose_out:
        o_ref[...] = out.T.astype(BF16)
    else:
        lane = lax.broadcasted_iota(jnp.int32, (1, LANES), 1)
        o_ref[...] = jnp.where(lane < NSA_HEAD_DIM, out, (lane == NSA_HEAD_DIM).astype(F32)).astype(BF16)


def _compress(t, pe, w1, w2, transpose_out):
    bsz, s_len, _ = t.shape
    hk, dk = NSA_KV_HEADS, NSA_HEAD_DIM
    n = s_len // CMP_STRIDE
    xr = t.reshape(bsz, n, CMP_STRIDE, hk, dk).transpose(0, 3, 1, 2, 4).reshape(bsz, hk, n, CMP_STRIDE * dk)
    pe8 = jnp.broadcast_to(pe.reshape(1, CMP_BLOCK * dk), (8, CMP_BLOCK * dk))
    w2p = jnp.pad(w2, ((0, 0), (0, LANES - dk))).astype(BF16)
    o_shape = (bsz, hk, LANES, n) if transpose_out else (bsz, hk, n, LANES)
    const = lambda b, h: (0, 0)
    return pl.pallas_call(
        functools.partial(_compress_kernel, transpose_out=transpose_out),
        grid=(bsz, hk),
        in_specs=[pl.BlockSpec((None, None, n, CMP_STRIDE * dk), lambda b, h: (b, h, 0, 0)),
                  pl.BlockSpec(pe8.shape, const),
                  pl.BlockSpec(w1.shape, const),
                  pl.BlockSpec(w2p.shape, const)],
        out_specs=pl.BlockSpec((None, None) + o_shape[2:], lambda b, h: (b, h, 0, 0)),
        out_shape=jax.ShapeDtypeStruct(o_shape, BF16),
        compiler_params=_params("parallel", "parallel"),
        name="compress",
    )(xr, pe8, w1.astype(BF16), w2p)


def _importance_matrix(n_cmp_rows, n_blocks):
    c = np.arange(n_cmp_rows)[:, None]
    j = np.arange(n_blocks)[None, :]
    per = SEL_BLOCK // CMP_STRIDE
    a = ((c >= per * j) & (c <= per * j + per - 1)).astype(np.float32)
    for r in range(1, CMP_BLOCK // CMP_STRIDE):
        a += ((c + r >= per * j) & (c + r <= per * j + per - 1)).astype(np.float32)
    return a


def _dot_split3(a, w):
    out = None
    for _ in range(3):
        part = a.astype(BF16)
        a = a - part.astype(F32)
        term = jnp.dot(part, w, preferred_element_type=F32)
        out = term if out is None else out + term
    return out


_SORT16 = ((0, 1), (2, 3), (0, 2), (1, 3), (1, 2), (4, 5), (6, 7), (4, 6), (5, 7), (5, 6), (0, 4), (2, 6), (2, 4),
           (1, 5), (3, 7), (3, 5), (1, 2), (3, 4), (5, 6), (8, 9), (10, 11), (8, 10), (9, 11), (9, 10), (12, 13),
           (14, 15), (12, 14), (13, 15), (13, 14), (8, 12), (10, 14), (10, 12), (9, 13), (11, 15), (11, 13), (9, 10),
           (11, 12), (13, 14), (0, 8), (4, 12), (4, 8), (2, 10), (6, 14), (6, 10), (2, 4), (6, 8), (10, 12), (1, 9),
           (5, 13), (5, 9), (3, 11), (7, 15), (7, 11), (3, 5), (7, 9), (11, 13), (1, 2), (3, 4), (5, 6), (7, 8),
           (9, 10), (11, 12), (13, 14))
_BITONIC16 = ((0, 8), (1, 9), (2, 10), (3, 11), (4, 12), (5, 13), (6, 14), (7, 15), (0, 4), (1, 5), (2, 6), (3, 7),
              (0, 2), (1, 3), (0, 1), (2, 3), (4, 6), (5, 7), (4, 5), (6, 7), (8, 12), (9, 13), (10, 14), (11, 15),
              (8, 10), (9, 11), (8, 9), (10, 11), (12, 14), (13, 15), (12, 13), (14, 15))


def _top_selection_bias(imp, blk0):
    nb = imp.shape[1]
    tl = lax.broadcasted_iota(jnp.int32, (Q_BLOCK, nb), 0)
    jb = lax.broadcasted_iota(jnp.int32, (Q_BLOCK, nb), 1)
    qblk = blk0 + tl // SEL_BLOCK
    forced = (jb == 0) | (jb == qblk) | (jb == qblk - 1)
    score = jnp.where(forced, jnp.inf, jnp.where(jb <= qblk, imp, -1.0))
    sc_t = score.T

    assert nb == N_SEL * 8
    v = [sc_t[8 * k:8 * k + 8] for k in range(N_SEL)]

    def exchange(net):
        for a_, b_ in net:
            hi, lo = jnp.maximum(v[a_], v[b_]), jnp.minimum(v[a_], v[b_])
            v[a_], v[b_] = hi, lo

    exchange(_SORT16)
    for shift in (4, 2):
        w = [pltpu.roll(x, shift, axis=0) for x in v]
        v = [jnp.maximum(v[k], w[N_SEL - 1 - k]) for k in range(N_SEL)]
        exchange(_BITONIC16)
    w = [pltpu.roll(x, 1, axis=0) for x in v]
    tau8 = jnp.maximum(v[0], w[N_SEL - 1])
    for k in range(1, N_SEL):
        tau8 = jnp.minimum(tau8, jnp.maximum(v[k], w[N_SEL - 1 - k]))
    tau = tau8[0:1]
    above = sc_t > tau
    tied = sc_t == tau
    room = N_SEL - jnp.sum(above.astype(F32), axis=0, keepdims=True)
    blk_r = lax.broadcasted_iota(jnp.int32, (nb, nb), 0)
    blk_c = lax.broadcasted_iota(jnp.int32, (nb, nb), 1)
    ties_before = jnp.dot((blk_c < blk_r).astype(BF16), tied.astype(BF16), preferred_element_type=F32)
    chosen = above | (tied & (ties_before < room))
    blk_t = lax.broadcasted_iota(jnp.int32, sc_t.shape, 0)
    qblk_t = blk0 + lax.broadcasted_iota(jnp.int32, sc_t.shape, 1) // SEL_BLOCK
    return jnp.where(chosen & (blk_t <= qblk_t), 0.0, -MASK_BIAS).T.astype(BF16)


def _nsa_kernel(qc_ref, qr_ref, gt_ref, kct_ref, vca_ref, ksa_ref, vsa_ref, kwt_ref, vwa_ref, imp_ref, dmask_ref,
                o_ref, sc_ref, part_ref, gsel_ref, *, n_qb):
    i0 = pl.program_id(2) * n_qb
    rows_qb = NSA_GROUP * Q_BLOCK
    rows = n_qb * rows_qb
    n_cmp = kct_ref.shape[1]
    qc = qc_ref[...].reshape(rows, LANES)
    qr = qr_ref[...].reshape(rows, LANES)
    row = lax.broadcasted_iota(jnp.int32, (rows, 1), 0)
    tok = row % Q_BLOCK
    t = (i0 + row // rows_qb) * Q_BLOCK + tok

    s = jnp.dot(qc, kct_ref[...], preferred_element_type=F32)
    cmp_end = lax.broadcasted_iota(jnp.int32, (1, n_cmp), 1) * CMP_STRIDE + (CMP_BLOCK - 1)
    s = jnp.where(cmp_end <= t, s, -jnp.inf)
    m = jnp.max(s, axis=-1, keepdims=True)
    m = jnp.where(m == -jnp.inf, 0.0, m)
    p = jnp.exp2(s - m)
    d = jnp.sum(p, axis=-1, keepdims=True)
    p = p * (1.0 / jnp.where(d > 0, d, 1.0))
    o_cmp = jnp.dot(p.astype(BF16), vca_ref[...], preferred_element_type=F32)

    cw = lax.broadcasted_iota(jnp.int32, (1, WIN_UNITS * Q_BLOCK), 1)
    tok_qb = tok[0:rows_qb]
    o_win = []
    for qb in range(n_qb):
        iq = i0 + qb
        units = [jnp.maximum(iq - (WIN_UNITS - 1) + u, 0) for u in range(WIN_UNITS)]
        kw = jnp.concatenate([kwt_ref[u] for u in units], axis=1)
        vw = jnp.concatenate([vwa_ref[u] for u in units], axis=0)
        sw = jnp.dot(qr[qb * rows_qb:(qb + 1) * rows_qb], kw, preferred_element_type=F32)
        wmask = (cw <= tok_qb + WINDOW) & (cw > tok_qb) & (cw // Q_BLOCK + iq >= WIN_UNITS - 1)
        sw = jnp.where(wmask, sw, -jnp.inf)
        pw = jnp.exp2(sw.astype(BF16) - jnp.max(sw, axis=-1, keepdims=True).astype(BF16))
        acc_w = jnp.dot(pw, vw, preferred_element_type=F32)
        o_win.append(acc_w * (1.0 / acc_w[:, NSA_HEAD_DIM:NSA_HEAD_DIM + 1]))
    o_win = jnp.concatenate(o_win, axis=0)

    gates = jax.nn.sigmoid(gt_ref[...])
    for qb in range(n_qb):
        gq = gates[qb * Q_BLOCK:(qb + 1) * Q_BLOCK]
        for g in range(NSA_GROUP):
            rs = slice(qb * rows_qb + g * Q_BLOCK, qb * rows_qb + (g + 1) * Q_BLOCK)
            c0 = g * N_BRANCH
            part_ref[rs, :] = gq[:, c0:c0 + 1] * o_cmp[rs] + gq[:, c0 + 2:c0 + 3] * o_win[rs]
            gsel_ref[rs, :] = jnp.broadcast_to(gq[:, c0 + 1:c0 + 2], (Q_BLOCK, LANES))

    biases = []
    for qb in range(n_qb):
        base = qb * rows_qb
        p_grp = p[base:base + Q_BLOCK]
        for g in range(1, NSA_GROUP):
            p_grp = p_grp + p[base + g * Q_BLOCK:base + (g + 1) * Q_BLOCK]
        imp = _dot_split3(p_grp, imp_ref[...])
        bias = _top_selection_bias(imp, (i0 + qb) * (Q_BLOCK // SEL_BLOCK))
        biases += [bias] * NSA_GROUP
    q_aug = jnp.concatenate([jnp.concatenate(biases, axis=0), qr], axis=1)

    def scores(ck, slot):
        sc_ref[slot] = jnp.dot(q_aug, ksa_ref[ck], preferred_element_type=F32).astype(BF16)

    def absorb(sc, ck, m_run, acc):
        m_new = jnp.maximum(m_run, jnp.max(sc, axis=-1, keepdims=True).astype(F32))
        pp = jnp.exp2(sc - m_new.astype(BF16))
        acc = jnp.exp2(m_run - m_new) * acc + jnp.dot(pp, vsa_ref[ck], preferred_element_type=F32)
        return m_new, acc

    last = (i0 * Q_BLOCK) // KEY_CHUNK

    def quad_body(kk, carry):
        for u in range(2):
            scores(4 * kk + 2 * u + 1, 1)
            carry = absorb(sc_ref[0], 4 * kk + 2 * u, *carry)
            scores(4 * kk + 2 * u + 2, 0)
            carry = absorb(sc_ref[1], 4 * kk + 2 * u + 1, *carry)
        return carry

    def pair_body(_, carry):
        base = 4 * (last // 4)
        scores(base + 1, 1)
        carry = absorb(sc_ref[0], base, *carry)
        scores(base + 2, 0)
        return absorb(sc_ref[1], base + 1, *carry)

    def odd_body(_, carry):
        scores(last, 1)
        return absorb(sc_ref[0], last - 1, *carry)

    scores(0, 0)
    carry = (jnp.full((rows, 1), -jnp.inf, F32), jnp.zeros((rows, LANES), F32))
    carry = lax.fori_loop(0, last // 4, quad_body, carry)
    carry = lax.fori_loop(0, (last % 4) // 2, pair_body, carry)
    carry = lax.fori_loop(0, last % 2, odd_body, carry)
    _, acc_s = absorb(sc_ref[last % 2] + dmask_ref[...], last, *carry)
    o_sel = acc_s * (1.0 / acc_s[:, NSA_HEAD_DIM:NSA_HEAD_DIM + 1])

    o = part_ref[...] + gsel_ref[...] * o_sel
    lane_lo = lax.broadcasted_iota(jnp.int32, (1, LANES), 1) < NSA_HEAD_DIM
    for qb in range(n_qb):
        heads = [o[qb * rows_qb + g * Q_BLOCK:qb * rows_qb + (g + 1) * Q_BLOCK] for g in range(NSA_GROUP)]
        pairs = [jnp.where(lane_lo, heads[2 * k], pltpu.roll(heads[2 * k + 1], NSA_HEAD_DIM, axis=1))
                 for k in range(NSA_GROUP // 2)]
        o_ref[qb * Q_BLOCK:(qb + 1) * Q_BLOCK, :] = jnp.concatenate(pairs, axis=1).astype(o_ref.dtype)


def _nsa_attention(qc, qr, gates_raw, kc_t, vc_aug, ks_aug, vs_aug, kw_t, vw_aug, imp_mat, n_qb=4):
    bsz, nq, hk, rows_qb, _ = qc.shape
    s_len = nq * Q_BLOCK
    n_ch = ks_aug.shape[2]
    n_units = kw_t.shape[2]
    n_cmp = kc_t.shape[3]
    assert nq % n_qb == 0 and KEY_CHUNK == n_qb * Q_BLOCK
    r = np.arange(n_qb * rows_qb)
    own = (r // rows_qb) * Q_BLOCK + r % Q_BLOCK
    dmask = jnp.asarray(np.where(np.arange(KEY_CHUNK)[None, :] <= own[:, None], 0.0, -np.inf), dtype=BF16)
    qspec = pl.BlockSpec((None, n_qb, None, rows_qb, LANES), lambda b, h, i: (b, i, h, 0, 0))
    per_head = lambda shp: pl.BlockSpec((None, None) + shp, lambda b, h, i: (b, h) + (0,) * len(shp))
    return pl.pallas_call(
        functools.partial(_nsa_kernel, n_qb=n_qb),
        grid=(bsz, hk, nq // n_qb),
        in_specs=[qspec, qspec,
                  pl.BlockSpec((None, n_qb * Q_BLOCK, LANES), lambda b, h, i: (b, i, h)),
                  per_head((LANES, n_cmp)),
                  per_head((n_cmp, LANES)),
                  per_head((n_ch, 2 * LANES, KEY_CHUNK)),
                  per_head((n_ch, KEY_CHUNK, LANES)),
                  per_head((n_units, LANES, Q_BLOCK)),
                  per_head((n_units, Q_BLOCK, LANES)),
                  pl.BlockSpec(imp_mat.shape, lambda b, h, i: (0, 0)),
                  pl.BlockSpec(dmask.shape, lambda b, h, i: (0, 0))],
        out_specs=pl.BlockSpec((None, n_qb * Q_BLOCK, NSA_GROUP * NSA_HEAD_DIM), lambda b, h, i: (b, i, h)),
        out_shape=jax.ShapeDtypeStruct((bsz, s_len, hk * NSA_GROUP * NSA_HEAD_DIM), BF16),
        scratch_shapes=[pltpu.VMEM((2, n_qb * rows_qb, KEY_CHUNK), BF16),
                        pltpu.VMEM((n_qb * rows_qb, LANES), F32),
                        pltpu.VMEM((n_qb * rows_qb, LANES), F32)],
        compiler_params=_params("parallel", "parallel", "arbitrary"),
        name="nsa_attention",
    )(qc, qr, gates_raw, kc_t, vc_aug, ks_aug, vs_aug, kw_t, vw_aug, imp_mat, dmask)


def kernel(x, c, mod_w, mod_b, norm_pre_mix, norm_post_mix, norm_pre_ffn, norm_post_ffn, ffn_w_in, ffn_w_out, ssm_w_in, ssm_conv_w, ssm_conv_b, ssm_dt_bias, ssm_a_log, ssm_d, ssm_norm, ssm_w_out, kv_norm, kv_mod_w, kv_mod_b, w_kv, cmp_pos_k, cmp_w1_k, cmp_w2_k, cmp_pos_v, cmp_w1_v, cmp_w2_v, nsa_w_q, nsa_w_o):
    bsz, s_len, d = x.shape
    depth = mod_w.shape[0]
    n_a = ssm_w_in.shape[0]
    d_ff = ffn_w_out.shape[1]
    d_inner = ssm_w_out.shape[1]
    n_ssm_heads = d_inner // SSM_HEADDIM
    d_xbc = ssm_conv_w.shape[-1]
    dq = NSA_HEADS * NSA_HEAD_DIM
    dkv = NSA_KV_HEADS * NSA_HEAD_DIM
    assert s_len % KEY_CHUNK == 0 and s_len // SEL_BLOCK <= LANES and s_len // SEL_BLOCK >= N_SEL

    mods = _mods(c, mod_w, mod_b)
    kv_mods = _mods(c, kv_mod_w[None], kv_mod_b[None])[0]
    vec = lambda m, k: m[:, k * d:(k + 1) * d].reshape(bsz, 1, d)

    shared = None
    tables = None
    for i in range(depth):
        sh_m, sc_m, g_m, sh_f, sc_f, g_f = [vec(mods[i], k) for k in range(6)]
        if i == n_a:
            tables = _rope_tables(s_len)
            w_kv_b = w_kv.astype(BF16)
            kc, vc, ks, vs, kw, vw = _adaln_mm(x, kv_norm, vec(kv_mods, 0), vec(kv_mods, 1),
                                               [w_kv_b[:, k * dkv:(k + 1) * dkv] for k in range(6)])
            kc_t = _compress(kc, cmp_pos_k, cmp_w1_k, cmp_w2_k, transpose_out=True)
            vc_aug = _compress(vc, cmp_pos_v, cmp_w1_v, cmp_w2_v, transpose_out=False)
            shared = (kc_t, vc_aug) + _kv_post(ks, vs, kw, vw, tables)
            imp_mat = jnp.asarray(_importance_matrix(s_len // CMP_STRIDE, LANES), dtype=BF16)
        if i < n_a:
            w_in = ssm_w_in[i].astype(BF16)
            w_dt = jnp.pad(w_in[:, d_inner + d_xbc:], ((0, 0), (0, LANES - n_ssm_heads)))
            z, xbc, dt_raw = _adaln_mm(x, norm_pre_mix[i], sh_m, sc_m,
                                       [w_in[:, :d_inner], w_in[:, d_inner:d_inner + d_xbc], w_dt])
            y = _ssd(xbc, z, dt_raw, ssm_conv_w[i], ssm_conv_b[i], ssm_dt_bias[i], ssm_a_log[i],
                     ssm_d[i], ssm_norm[i])
            x = _mm_post(y, ssm_w_out[i].astype(BF16), x, g_m, norm_post_mix[i])
        else:
            w_q = nsa_w_q[i - n_a].astype(BF16)
            w_g = w_q[:, dq:].reshape(d, NSA_KV_HEADS, NSA_GROUP * N_BRANCH)
            w_g = jnp.pad(w_g, ((0, 0), (0, 0), (0, LANES - NSA_GROUP * N_BRANCH))).reshape(d, NSA_KV_HEADS * LANES)
            qc, qr, gates_raw = _q_proj(x, norm_pre_mix[i], sh_m, sc_m, w_q[:, :dq], w_g, tables)
            o = _nsa_attention(qc, qr, gates_raw, *shared, imp_mat)
            x = _mm_post(o, nsa_w_o[i - n_a].astype(BF16), x, g_m, norm_post_mix[i])
        w_ffn = ffn_w_in[i].astype(BF16)
        x = _ffn(x, norm_pre_ffn[i], sh_f, sc_f, w_ffn[:, :d_ff], w_ffn[:, d_ff:], ffn_w_out[i].astype(BF16),
                 norm_post_ffn[i], g_f)
    return x
```

```python
import functools
import math

import numpy as np
import jax
import jax.numpy as jnp
from jax import lax
from jax.experimental import pallas as pl
from jax.experimental.pallas import tpu as pltpu

F32 = jnp.float32
BF16 = jnp.bfloat16

EPS = 1e-6
DEPTH = 4
N_A = DEPTH // 2

SSM_HEADDIM = 64
SSM_STATE = 128
SSM_GROUPS = 4
CONV_W = 4
SSD_CHUNK = 256

NSA_HEADS = 16
NSA_KV_HEADS = 4
NSA_GROUP = NSA_HEADS // NSA_KV_HEADS
NSA_HEAD_DIM = 64
CMP_BLOCK = 32
CMP_STRIDE = 16
SEL_BLOCK = 64
N_SEL = 16
WINDOW = 512
Q_BLOCK = 128
N_BRANCH = 3
ROPE_THETA = 500000.0
ROT_DIMS = NSA_HEAD_DIM // 4

LANES = 128
KEY_CHUNK = 512
WIN_UNITS = WINDOW // Q_BLOCK + 1
MASK_BIAS = float(2 ** 30)
VMEM_LIMIT = 56 * 1024 * 1024


def _params(*sem):
    return pltpu.CompilerParams(dimension_semantics=sem, vmem_limit_bytes=VMEM_LIMIT)


def _rms(x, g):
    return x * lax.rsqrt(jnp.mean(x * x, axis=-1, keepdims=True) + EPS) * g


def _silu(x):
    h = 0.5 * x
    return h + h * jnp.tanh(h)


def _mods_kernel(c_ref, w_ref, b_ref, o_ref):
    cs = _silu(c_ref[...])
    o_ref[...] = jnp.dot(cs, w_ref[...], preferred_element_type=F32) + b_ref[...]


def _mods(c, w, b):
    n_l, d, n = w.shape
    bsz = c.shape[0]
    tn = 1024
    return pl.pallas_call(
        _mods_kernel,
        grid=(n_l, n // tn),
        in_specs=[pl.BlockSpec((bsz, d), lambda l, j: (0, 0)),
                  pl.BlockSpec((None, d, tn), lambda l, j: (l, 0, j)),
                  pl.BlockSpec((None, 1, tn), lambda l, j: (l, 0, j))],
        out_specs=pl.BlockSpec((None, bsz, tn), lambda l, j: (l, 0, j)),
        out_shape=jax.ShapeDtypeStruct((n_l, bsz, n), F32),
        compiler_params=_params("parallel", "parallel"),
        name="mods",
    )(c, w, b.reshape(n_l, 1, n))


def _adaln_mm_kernel(x_ref, g_ref, sh_ref, sc_ref, *refs, n_w):
    h = _rms(x_ref[...], g_ref[...]) * (1.0 + sc_ref[...]) + sh_ref[...]
    hb = h.astype(BF16)
    for w_ref, o_ref in zip(refs[:n_w], refs[n_w:]):
        o_ref[...] = jnp.dot(hb, w_ref[...], preferred_element_type=F32).astype(o_ref.dtype)


def _adaln_mm(x, g, shift, scale, ws, tm=512):
    bsz, s_len, d = x.shape
    n_w = len(ws)
    const = lambda b, i: (0, 0)
    return pl.pallas_call(
        functools.partial(_adaln_mm_kernel, n_w=n_w),
        grid=(bsz, s_len // tm),
        in_specs=[pl.BlockSpec((None, tm, d), lambda b, i: (b, i, 0)),
                  pl.BlockSpec((1, d), const),
                  pl.BlockSpec((None, 1, d), lambda b, i: (b, 0, 0)),
                  pl.BlockSpec((None, 1, d), lambda b, i: (b, 0, 0))]
                 + [pl.BlockSpec(w.shape, const) for w in ws],
        out_specs=[pl.BlockSpec((None, tm, w.shape[1]), lambda b, i: (b, i, 0)) for w in ws],
        out_shape=[jax.ShapeDtypeStruct((bsz, s_len, w.shape[1]), F32) for w in ws],
        compiler_params=_params("parallel", "parallel"),
        name="adaln_mm",
    )(x, g.reshape(1, d), shift, scale, *ws)


def _mm_post_kernel(a_ref, w_ref, x_ref, gate_ref, g_ref, o_ref):
    y = jnp.dot(a_ref[...], w_ref[...], preferred_element_type=F32)
    o_ref[...] = x_ref[...] + gate_ref[...] * _rms(y, g_ref[...])


def _mm_post(a, w, x, gate, g, tm=1024):
    bsz, s_len, d = x.shape
    k = a.shape[-1]
    return pl.pallas_call(
        _mm_post_kernel,
        grid=(bsz, s_len // tm),
        in_specs=[pl.BlockSpec((None, tm, k), lambda b, i: (b, i, 0)),
                  pl.BlockSpec((k, d), lambda b, i: (0, 0)),
                  pl.BlockSpec((None, tm, d), lambda b, i: (b, i, 0)),
                  pl.BlockSpec((None, 1, d), lambda b, i: (b, 0, 0)),
                  pl.BlockSpec((1, d), lambda b, i: (0, 0))],
        out_specs=pl.BlockSpec((None, tm, d), lambda b, i: (b, i, 0)),
        out_shape=jax.ShapeDtypeStruct(x.shape, F32),
        compiler_params=_params("parallel", "parallel"),
        name="mm_post",
    )(a, w, x, gate, g.reshape(1, d))


def _ffn_kernel(x_ref, g1_ref, sh_ref, sc_ref, wa_ref, wb_ref, wo_ref, g2_ref, gate_ref, o_ref, u_ref, *, tc):
    x = x_ref[...]
    hb = (_rms(x, g1_ref[...]) * (1.0 + sc_ref[...]) + sh_ref[...]).astype(BF16)
    d_ff = wa_ref.shape[1]
    for c0 in range(0, d_ff, tc):
        a = jnp.dot(hb, wa_ref[:, c0:c0 + tc], preferred_element_type=F32)
        b = jnp.dot(hb, wb_ref[:, c0:c0 + tc], preferred_element_type=F32)
        u_ref[:, c0:c0 + tc] = (_silu(a) * b).astype(BF16)
    y = jnp.dot(u_ref[...], wo_ref[...], preferred_element_type=F32)
    o_ref[...] = x + gate_ref[...] * _rms(y, g2_ref[...])


def _ffn(x, g1, shift, scale, wa, wb, wo, g2, gate, tm=512, tc=256):
    bsz, s_len, d = x.shape
    d_ff = wa.shape[1]
    const = lambda b, i: (0, 0)
    vec = lambda b, i: (b, 0, 0)
    return pl.pallas_call(
        functools.partial(_ffn_kernel, tc=tc),
        grid=(bsz, s_len // tm),
        in_specs=[pl.BlockSpec((None, tm, d), lambda b, i: (b, i, 0)),
                  pl.BlockSpec((1, d), const),
                  pl.BlockSpec((None, 1, d), vec),
                  pl.BlockSpec((None, 1, d), vec),
                  pl.BlockSpec((d, d_ff), const),
                  pl.BlockSpec((d, d_ff), const),
                  pl.BlockSpec((d_ff, d), const),
                  pl.BlockSpec((1, d), const),
                  pl.BlockSpec((None, 1, d), vec)],
        out_specs=pl.BlockSpec((None, tm, d), lambda b, i: (b, i, 0)),
        out_shape=jax.ShapeDtypeStruct(x.shape, F32),
        scratch_shapes=[pltpu.VMEM((tm, d_ff), BF16)],
        compiler_params=_params("parallel", "parallel"),
        name="ffn",
    )(x, g1.reshape(1, d), shift, scale, wa, wb, wo, g2.reshape(1, d), gate)


def _pair(mat, p, lane_lo):
    return jnp.where(lane_lo, mat[:, 2 * p:2 * p + 1], mat[:, 2 * p + 1:2 * p + 2])


def _ssd_kernel(xbc_ref, z_ref, dt_ref, cw_ref, cb_ref, dtb_ref, alog_ref, dsk_ref, ng_ref, o_ref,
                ext_ref, xs_ref, state_ref, y_ref, *, d_inner, n_groups):
    L = SSD_CHUNK
    N = SSM_STATE
    gw = d_inner // n_groups
    pairs_per_group = gw // LANES
    n_pairs = d_inner // LANES
    c_idx = pl.program_id(1)

    @pl.when(c_idx == 0)
    def _():
        ext_ref[0:8, :] = jnp.zeros((8, ext_ref.shape[1]), F32)
        state_ref[...] = jnp.zeros(state_ref.shape, F32)

    ext_ref[8:8 + L, :] = xbc_ref[...]
    d_xbc = ext_ref.shape[1]
    bc = []
    for c0 in range(0, d_xbc, 512):
        acc = cb_ref[:, c0:c0 + 512] + cw_ref[0:1, c0:c0 + 512] * ext_ref[5:5 + L, c0:c0 + 512]
        for k in range(1, CONV_W):
            acc = acc + cw_ref[k:k + 1, c0:c0 + 512] * ext_ref[5 + k:5 + k + L, c0:c0 + 512]
        acc = _silu(acc)
        if c0 < d_inner:
            xs_ref[:, c0:c0 + 512] = acc
        else:
            bc.append(acc)
    ext_ref[0:8, :] = ext_ref[L:L + 8, :]
    b_all = jnp.concatenate(bc[:len(bc) // 2], axis=1)
    c_all = jnp.concatenate(bc[len(bc) // 2:], axis=1)

    x_raw = dt_ref[...] + dtb_ref[...]
    dtv = jnp.maximum(x_raw, 0.0) + jnp.log1p(jnp.exp(-jnp.abs(x_raw)))
    da = dtv * (-jnp.exp(alog_ref[...]))
    ri = lax.broadcasted_iota(jnp.int32, (L, L), 0)
    ci = lax.broadcasted_iota(jnp.int32, (L, L), 1)
    causal = ci <= ri
    acs = jnp.dot(causal.astype(F32), da, preferred_element_type=F32, precision=lax.Precision.HIGHEST)
    acs_t = acs.T
    acs_last = acs[L - 1:L, :]
    e_acs = jnp.exp(acs)
    tail = jnp.exp(acs_last - acs)
    e_last = jnp.exp(acs_last)
    lane_lo = lax.broadcasted_iota(jnp.int32, (1, LANES), 1) < SSM_HEADDIM

    cb = None
    for p in range(n_pairs):
        g = p // pairs_per_group
        if p % pairs_per_group == 0:
            bg = b_all[:, g * N:(g + 1) * N]
            cg = c_all[:, g * N:(g + 1) * N].astype(BF16)
            cb = lax.dot_general(cg, bg.astype(BF16), (((1,), (1,)), ((), ())), preferred_element_type=F32)
            bg_t = bg.T.astype(BF16)
        x2 = xs_ref[:, p * LANES:(p + 1) * LANES]
        xdt = x2 * _pair(dtv, p, lane_lo)
        y2 = None
        for half in range(2):
            h = 2 * p + half
            seg = acs[:, h:h + 1] - acs_t[h:h + 1, :]
            m = (cb * jnp.exp(jnp.where(causal, seg, -jnp.inf))).astype(BF16)
            keep = lane_lo if half == 0 else jnp.logical_not(lane_lo)
            part = jnp.dot(m, jnp.where(keep, xdt, 0.0).astype(BF16), preferred_element_type=F32)
            y2 = part if y2 is None else y2 + part
        st = state_ref[:, p * LANES:(p + 1) * LANES]
        y2 = y2 + jnp.dot(cg, st.astype(BF16), preferred_element_type=F32) * _pair(e_acs, p, lane_lo)
        y2 = y2 + dsk_ref[:, p * LANES:(p + 1) * LANES] * x2
        y_ref[:, p * LANES:(p + 1) * LANES] = y2
        w2 = (xdt * _pair(tail, p, lane_lo)).astype(BF16)
        state_ref[:, p * LANES:(p + 1) * LANES] = (
            st * _pair(e_last, p, lane_lo) + jnp.dot(bg_t, w2, preferred_element_type=F32))

    for g in range(n_groups):
        sl = slice(g * gw, (g + 1) * gw)
        yg = y_ref[:, sl] * _silu(z_ref[:, sl])
        o_ref[:, sl] = _rms(yg, ng_ref[:, sl]).astype(o_ref.dtype)


def _ssd(xbc, z, dt_raw, conv_w, conv_b, dt_bias, a_log, d_skip, norm_g):
    bsz, s_len, d_xbc = xbc.shape
    d_inner = z.shape[-1]
    n_heads = d_inner // SSM_HEADDIM
    L = SSD_CHUNK
    pad = LANES - n_heads
    row = lambda v: jnp.pad(v.astype(F32), (0, pad)).reshape(1, LANES)
    const = lambda b, c: (0, 0)
    blk = lambda w: pl.BlockSpec((None, L, w), lambda b, c: (b, c, 0))
    return pl.pallas_call(
        functools.partial(_ssd_kernel, d_inner=d_inner, n_groups=SSM_GROUPS),
        grid=(bsz, s_len // L),
        in_specs=[blk(d_xbc), blk(d_inner), blk(LANES),
                  pl.BlockSpec((CONV_W, d_xbc), const),
                  pl.BlockSpec((1, d_xbc), const),
                  pl.BlockSpec((1, LANES), const),
                  pl.BlockSpec((1, LANES), const),
                  pl.BlockSpec((1, d_inner), const),
                  pl.BlockSpec((1, d_inner), const)],
        out_specs=blk(d_inner),
        out_shape=jax.ShapeDtypeStruct((bsz, s_len, d_inner), BF16),
        scratch_shapes=[pltpu.VMEM((L + 8, d_xbc), F32),
                        pltpu.VMEM((L, d_inner), F32),
                        pltpu.VMEM((SSM_STATE, d_inner), F32),
                        pltpu.VMEM((L, d_inner), F32)],
        compiler_params=_params("parallel", "arbitrary"),
        name="ssd",
    )(xbc, z, dt_raw, conv_w, conv_b.reshape(1, d_xbc), row(dt_bias), row(a_log),
      jnp.repeat(d_skip.astype(F32), SSM_HEADDIM).reshape(1, d_inner), norm_g.reshape(1, d_inner))


def _rope_tables(s_len):
    half = ROT_DIMS // 2
    inv_freq = ROPE_THETA ** (-jnp.arange(half, dtype=F32) / half)
    ang = jnp.arange(s_len).astype(F32)[:, None] * inv_freq[None, :]
    cos, sin = jnp.cos(ang), jnp.sin(ang)
    rest = NSA_HEAD_DIM - ROT_DIMS
    one = jnp.ones((s_len, rest), F32)
    zero = jnp.zeros((s_len, rest), F32)
    zh = jnp.zeros((s_len, half), F32)
    t_cos = jnp.concatenate([cos, cos, one], axis=1)
    t_up = jnp.concatenate([-sin, zh, zero], axis=1)
    t_dn = jnp.concatenate([zh, sin, zero], axis=1)
    two = lambda t: jnp.concatenate([t, t], axis=1)
    return two(t_cos), two(t_up), two(t_dn)


def _rope(x, t_cos, t_up, t_dn):
    half = ROT_DIMS // 2
    return x * t_cos + pltpu.roll(x, LANES - half, axis=1) * t_up + pltpu.roll(x, half, axis=1) * t_dn


def _q_proj_kernel(x_ref, g_ref, sh_ref, sc_ref, wq_ref, wg_ref, tc_ref, tu_ref, td_ref, qc_ref, qr_ref, gt_ref):
    h = _rms(x_ref[...], g_ref[...]) * (1.0 + sc_ref[...]) + sh_ref[...]
    hb = h.astype(BF16)
    gt_ref[...] = jnp.dot(hb, wg_ref[...], preferred_element_type=F32)
    q = jnp.dot(hb, wq_ref[...], preferred_element_type=F32)
    scale = NSA_HEAD_DIM ** -0.5 * math.log2(math.e)
    lane_lo = lax.broadcasted_iota(jnp.int32, (1, LANES), 1) < NSA_HEAD_DIM
    for blk in range(qc_ref.shape[0]):
        rs = slice(blk * Q_BLOCK, (blk + 1) * Q_BLOCK)
        for col in range(q.shape[1] // LANES):
            x = q[rs, col * LANES:(col + 1) * LANES] * scale
            xr = _rope(x, tc_ref[rs, :], tu_ref[rs, :], td_ref[rs, :])
            for half in range(2):
                head = 2 * col + half
                kvh, g = head // NSA_GROUP, head % NSA_GROUP
                for src, dst in ((x, qc_ref), (xr, qr_ref)):
                    v = src if half == 0 else pltpu.roll(src, NSA_HEAD_DIM, axis=1)
                    dst[blk, kvh, g * Q_BLOCK:(g + 1) * Q_BLOCK, :] = jnp.where(lane_lo, v, 0.0).astype(BF16)


def _q_proj(x, g, shift, scale, w_q, w_g, tables, n_blk=4):
    bsz, s_len, d = x.shape
    nq = s_len // Q_BLOCK
    tm = n_blk * Q_BLOCK
    const = lambda b, i: (0, 0)
    vec = lambda b, i: (b, 0, 0)
    tab = pl.BlockSpec((tm, LANES), lambda b, i: (i, 0))
    out = pl.BlockSpec((None, n_blk, NSA_KV_HEADS, NSA_GROUP * Q_BLOCK, LANES), lambda b, i: (b, i, 0, 0, 0))
    shape = jax.ShapeDtypeStruct((bsz, nq, NSA_KV_HEADS, NSA_GROUP * Q_BLOCK, LANES), BF16)
    return pl.pallas_call(
        _q_proj_kernel,
        grid=(bsz, nq // n_blk),
        in_specs=[pl.BlockSpec((None, tm, d), lambda b, i: (b, i, 0)),
                  pl.BlockSpec((1, d), const),
                  pl.BlockSpec((None, 1, d), vec),
                  pl.BlockSpec((None, 1, d), vec),
                  pl.BlockSpec(w_q.shape, const),
                  pl.BlockSpec(w_g.shape, const),
                  tab, tab, tab],
        out_specs=[out, out, pl.BlockSpec((None, tm, w_g.shape[1]), lambda b, i: (b, i, 0))],
        out_shape=[shape, shape, jax.ShapeDtypeStruct((bsz, s_len, w_g.shape[1]), F32)],
        compiler_params=_params("parallel", "parallel"),
        name="q_proj",
    )(x, g.reshape(1, d), shift, scale, w_q, w_g, *tables)


def _kv_post_kernel(ks_ref, vs_ref, kw_ref, vw_ref, tc_ref, tu_ref, td_ref,
                    ksa_ref, vsa_ref, kwt_ref, vwa_ref):
    tm = ks_ref.shape[0]
    chunk = pl.program_id(1)
    lane = lax.broadcasted_iota(jnp.int32, (1, LANES), 1)
    lane_lo = lane < NSA_HEAD_DIM
    ones_col = (lane == NSA_HEAD_DIM).astype(F32)
    row_lo = lax.broadcasted_iota(jnp.int32, (LANES, 1), 0) < NSA_HEAD_DIM
    blk_row = lax.broadcasted_iota(jnp.int32, (LANES, tm), 0)
    key_blk = chunk * (tm // SEL_BLOCK) + lax.broadcasted_iota(jnp.int32, (LANES, tm), 1) // SEL_BLOCK
    expand = (blk_row == key_blk).astype(BF16)
    for col in range(ks_ref.shape[1] // LANES):
        sl = slice(col * LANES, (col + 1) * LANES)
        ks_t = _rope(ks_ref[:, sl], tc_ref[...], tu_ref[...], td_ref[...]).T
        kw_t = _rope(kw_ref[:, sl], tc_ref[...], tu_ref[...], td_ref[...]).T
        vs, vw = vs_ref[:, sl], vw_ref[:, sl]
        for half in range(2):
            h = 2 * col + half
            if half == 1:
                ks_t = pltpu.roll(ks_t, NSA_HEAD_DIM, axis=0)
                kw_t = pltpu.roll(kw_t, NSA_HEAD_DIM, axis=0)
                vs = pltpu.roll(vs, NSA_HEAD_DIM, axis=1)
                vw = pltpu.roll(vw, NSA_HEAD_DIM, axis=1)
            ksa_ref[h, 0:LANES, :] = expand
            ksa_ref[h, LANES:2 * LANES, :] = jnp.where(row_lo, ks_t, 0.0).astype(BF16)
            kw_pad = jnp.where(row_lo, kw_t, 0.0).astype(BF16)
            for u in range(tm // Q_BLOCK):
                kwt_ref[h, u] = kw_pad[:, u * Q_BLOCK:(u + 1) * Q_BLOCK]
            vsa_ref[h] = jnp.where(lane_lo, vs, ones_col).astype(BF16)
            vwa_ref[h] = jnp.where(lane_lo, vw, ones_col).astype(BF16)


def _kv_post(ks, vs, kw, vw, tables):
    bsz, s_len, dk = ks.shape
    tm = KEY_CHUNK
    n_ch = s_len // tm
    upc = tm // Q_BLOCK
    inp = pl.BlockSpec((None, tm, dk), lambda b, c: (b, c, 0))
    tab = pl.BlockSpec((tm, LANES), lambda b, c: (c, 0))
    hk = NSA_KV_HEADS
    outs = pl.pallas_call(
        _kv_post_kernel,
        grid=(bsz, n_ch),
        in_specs=[inp, inp, inp, inp, tab, tab, tab],
        out_specs=[pl.BlockSpec((None, hk, None, 2 * LANES, tm), lambda b, c: (b, 0, c, 0, 0)),
                   pl.BlockSpec((None, hk, None, tm, LANES), lambda b, c: (b, 0, c, 0, 0)),
                   pl.BlockSpec((None, hk, upc, LANES, Q_BLOCK), lambda b, c: (b, 0, c, 0, 0)),
                   pl.BlockSpec((None, hk, None, tm, LANES), lambda b, c: (b, 0, c, 0, 0))],
        out_shape=[jax.ShapeDtypeStruct((bsz, hk, n_ch, 2 * LANES, tm), BF16),
                   jax.ShapeDtypeStruct((bsz, hk, n_ch, tm, LANES), BF16),
                   jax.ShapeDtypeStruct((bsz, hk, n_ch * upc, LANES, Q_BLOCK), BF16),
                   jax.ShapeDtypeStruct((bsz, hk, n_ch, tm, LANES), BF16)],
        compiler_params=_params("parallel", "parallel"),
        name="kv_post",
    )(ks, vs, kw, vw, *tables)
    ks_aug, vs_aug, kw_t, vw_aug = outs
    return ks_aug, vs_aug, kw_t, vw_aug.reshape(bsz, hk, n_ch * upc, Q_BLOCK, LANES)


def _compress_kernel(x_ref, pe_ref, w1_ref, w2_ref, o_ref, *, transpose_out):
    half = w1_ref.shape[0] // 2
    xb = x_ref[...].astype(BF16)
    y0 = jnp.dot(xb, w1_ref[0:half, :], preferred_element_type=F32)
    y1 = jnp.dot(xb, w1_ref[half:2 * half, :], preferred_element_type=F32)
    bias = jnp.dot(pe_ref[...].astype(BF16), w1_ref[...], preferred_element_type=F32)[0:1, :]
    n = y0.shape[0]
    hid = y0 + pltpu.roll(y1, n - 1, axis=0) + bias
    out = jnp.dot(_silu(hid).astype(BF16), w2_ref[...], preferred_element_type=F32)
    if transpose_out:
        o_ref[...] = out.T.astype(BF16)
    else:
        lane = lax.broadcasted_iota(jnp.int32, (1, LANES), 1)
        o_ref[...] = jnp.where(lane < NSA_HEAD_DIM, out, (lane == NSA_HEAD_DIM).astype(F32)).astype(BF16)


def _compress(t, pe, w1, w2, transpose_out):
    bsz, s_len, _ = t.shape
    hk, dk = NSA_KV_HEADS, NSA_HEAD_DIM
    n = s_len // CMP_STRIDE
    xr = t.reshape(bsz, n, CMP_STRIDE, hk, dk).transpose(0, 3, 1, 2, 4).reshape(bsz, hk, n, CMP_STRIDE * dk)
    pe8 = jnp.broadcast_to(pe.reshape(1, CMP_BLOCK * dk), (8, CMP_BLOCK * dk))
    w2p = jnp.pad(w2, ((0, 0), (0, LANES - dk))).astype(BF16)
    o_shape = (bsz, hk, LANES, n) if transpose_out else (bsz, hk, n, LANES)
    const = lambda b, h: (0, 0)
    return pl.pallas_call(
        functools.partial(_compress_kernel, transpose_out=transpose_out),
        grid=(bsz, hk),
        in_specs=[pl.BlockSpec((None, None, n, CMP_STRIDE * dk), lambda b, h: (b, h, 0, 0)),
                  pl.BlockSpec(pe8.shape, const),
                  pl.BlockSpec(w1.shape, const),
                  pl.BlockSpec(w2p.shape, const)],
        out_specs=pl.BlockSpec((None, None) + o_shape[2:], lambda b, h: (b, h, 0, 0)),
        out_shape=jax.ShapeDtypeStruct(o_shape, BF16),
        compiler_params=_params("parallel", "parallel"),
        name="compress",
    )(xr, pe8, w1.astype(BF16), w2p)


def _importance_matrix(n_cmp_rows, n_blocks):
    c = np.arange(n_cmp_rows)[:, None]
    j = np.arange(n_blocks)[None, :]
    per = SEL_BLOCK // CMP_STRIDE
    a = ((c >= per * j) & (c <= per * j + per - 1)).astype(np.float32)
    for r in range(1, CMP_BLOCK // CMP_STRIDE):
        a += ((c + r >= per * j) & (c + r <= per * j + per - 1)).astype(np.float32)
    return a


def _dot_split3(a, w):
    out = None
    for _ in range(3):
        part = a.astype(BF16)
        a = a - part.astype(F32)
        term = jnp.dot(part, w, preferred_element_type=F32)
        out = term if out is None else out + term
    return out


_SORT16 = ((0, 1), (2, 3), (0, 2), (1, 3), (1, 2), (4, 5), (6, 7), (4, 6), (5, 7), (5, 6), (0, 4), (2, 6), (2, 4),
           (1, 5), (3, 7), (3, 5), (1, 2), (3, 4), (5, 6), (8, 9), (10, 11), (8, 10), (9, 11), (9, 10), (12, 13),
           (14, 15), (12, 14), (13, 15), (13, 14), (8, 12), (10, 14), (10, 12), (9, 13), (11, 15), (11, 13), (9, 10),
           (11, 12), (13, 14), (0, 8), (4, 12), (4, 8), (2, 10), (6, 14), (6, 10), (2, 4), (6, 8), (10, 12), (1, 9),
           (5, 13), (5, 9), (3, 11), (7, 15), (7, 11), (3, 5), (7, 9), (11, 13), (1, 2), (3, 4), (5, 6), (7, 8),
           (9, 10), (11, 12), (13, 14))
_BITONIC16 = ((0, 8), (1, 9), (2, 10), (3, 11), (4, 12), (5, 13), (6, 14), (7, 15), (0, 4), (1, 5), (2, 6), (3, 7),
              (0, 2), (1, 3), (0, 1), (2, 3), (4, 6), (5, 7), (4, 5), (6, 7), (8, 12), (9, 13), (10, 14), (11, 15),
              (8, 10), (9, 11), (8, 9), (10, 11), (12, 14), (13, 15), (12, 13), (14, 15))


def _top_selection_bias(imp, blk0):
    nb = imp.shape[1]
    tl = lax.broadcasted_iota(jnp.int32, (Q_BLOCK, nb), 0)
    jb = lax.broadcasted_iota(jnp.int32, (Q_BLOCK, nb), 1)
    qblk = blk0 + tl // SEL_BLOCK
    forced = (jb == 0) | (jb == qblk) | (jb == qblk - 1)
    score = jnp.where(forced, jnp.inf, jnp.where(jb <= qblk, imp, -1.0))
    sc_t = score.T

    assert nb == N_SEL * 8
    v = [sc_t[8 * k:8 * k + 8] for k in range(N_SEL)]

    def exchange(net):
        for a_, b_ in net:
            hi, lo = jnp.maximum(v[a_], v[b_]), jnp.minimum(v[a_], v[b_])
            v[a_], v[b_] = hi, lo

    exchange(_SORT16)
    for shift in (4, 2):
        w = [pltpu.roll(x, shift, axis=0) for x in v]
        v = [jnp.maximum(v[k], w[N_SEL - 1 - k]) for k in range(N_SEL)]
        exchange(_BITONIC16)
    w = [pltpu.roll(x, 1, axis=0) for x in v]
    tau8 = jnp.maximum(v[0], w[N_SEL - 1])
    for k in range(1, N_SEL):
        tau8 = jnp.minimum(tau8, jnp.maximum(v[k], w[N_SEL - 1 - k]))
    tau = tau8[0:1]
    above = sc_t > tau
    tied = sc_t == tau
    room = N_SEL - jnp.sum(above.astype(F32), axis=0, keepdims=True)
    blk_r = lax.broadcasted_iota(jnp.int32, (nb, nb), 0)
    blk_c = lax.broadcasted_iota(jnp.int32, (nb, nb), 1)
    ties_before = jnp.dot((blk_c < blk_r).astype(BF16), tied.astype(BF16), preferred_element_type=F32)
    chosen = above | (tied & (ties_before < room))
    blk_t = lax.broadcasted_iota(jnp.int32, sc_t.shape, 0)
    qblk_t = blk0 + lax.broadcasted_iota(jnp.int32, sc_t.shape, 1) // SEL_BLOCK
    return jnp.where(chosen & (blk_t <= qblk_t), 0.0, -MASK_BIAS).T.astype(BF16)


def _nsa_kernel(qc_ref, qr_ref, gt_ref, kct_ref, vca_ref, ksa_ref, vsa_ref, kwt_ref, vwa_ref, imp_ref, dmask_ref,
                o_ref, sc_ref, part_ref, gsel_ref, *, n_qb):
    i0 = pl.program_id(2) * n_qb
    rows_qb = NSA_GROUP * Q_BLOCK
    rows = n_qb * rows_qb
    n_cmp = kct_ref.shape[1]
    qc = qc_ref[...].reshape(rows, LANES)
    qr = qr_ref[...].reshape(rows, LANES)
    row = lax.broadcasted_iota(jnp.int32, (rows, 1), 0)
    tok = row % Q_BLOCK
    t = (i0 + row // rows_qb) * Q_BLOCK + tok

    s = jnp.dot(qc, kct_ref[...], preferred_element_type=F32)
    cmp_end = lax.broadcasted_iota(jnp.int32, (1, n_cmp), 1) * CMP_STRIDE + (CMP_BLOCK - 1)
    s = jnp.where(cmp_end <= t, s, -jnp.inf)
    m = jnp.max(s, axis=-1, keepdims=True)
    m = jnp.where(m == -jnp.inf, 0.0, m)
    p = jnp.exp2(s - m)
    d = jnp.sum(p, axis=-1, keepdims=True)
    p = p * (1.0 / jnp.where(d > 0, d, 1.0))
    o_cmp = jnp.dot(p.astype(BF16), vca_ref[...], preferred_element_type=F32)

    cw = lax.broadcasted_iota(jnp.int32, (1, WIN_UNITS * Q_BLOCK), 1)
    tok_qb = tok[0:rows_qb]
    o_win = []
    for qb in range(n_qb):
        iq = i0 + qb
        units = [jnp.maximum(iq - (WIN_UNITS - 1) + u, 0) for u in range(WIN_UNITS)]
        kw = jnp.concatenate([kwt_ref[u] for u in units], axis=1)
        vw = jnp.concatenate([vwa_ref[u] for u in units], axis=0)
        sw = jnp.dot(qr[qb * rows_qb:(qb + 1) * rows_qb], kw, preferred_element_type=F32)
        wmask = (cw <= tok_qb + WINDOW) & (cw > tok_qb) & (cw // Q_BLOCK + iq >= WIN_UNITS - 1)
        sw = jnp.where(wmask, sw, -jnp.inf)
        pw = jnp.exp2(sw.astype(BF16) - jnp.max(sw, axis=-1, keepdims=True).astype(BF16))
        acc_w = jnp.dot(pw, vw, preferred_element_type=F32)
        o_win.append(acc_w * (1.0 / acc_w[:, NSA_HEAD_DIM:NSA_HEAD_DIM + 1]))
    o_win = jnp.concatenate(o_win, axis=0)

    gates = jax.nn.sigmoid(gt_ref[...])
    for qb in range(n_qb):
        gq = gates[qb * Q_BLOCK:(qb + 1) * Q_BLOCK]
        for g in range(NSA_GROUP):
            rs = slice(qb * rows_qb + g * Q_BLOCK, qb * rows_qb + (g + 1) * Q_BLOCK)
            c0 = g * N_BRANCH
            part_ref[rs, :] = gq[:, c0:c0 + 1] * o_cmp[rs] + gq[:, c0 + 2:c0 + 3] * o_win[rs]
            gsel_ref[rs, :] = jnp.broadcast_to(gq[:, c0 + 1:c0 + 2], (Q_BLOCK, LANES))

    biases = []
    for qb in range(n_qb):
        base = qb * rows_qb
        p_grp = p[base:base + Q_BLOCK]
        for g in range(1, NSA_GROUP):
            p_grp = p_grp + p[base + g * Q_BLOCK:base + (g + 1) * Q_BLOCK]
        imp = _dot_split3(p_grp, imp_ref[...])
        bias = _top_selection_bias(imp, (i0 + qb) * (Q_BLOCK // SEL_BLOCK))
        biases += [bias] * NSA_GROUP
    q_aug = jnp.concatenate([jnp.concatenate(biases, axis=0), qr], axis=1)

    def scores(ck, slot):
        sc_ref[slot] = jnp.dot(q_aug, ksa_ref[ck], preferred_element_type=F32).astype(BF16)

    def absorb(sc, ck, m_run, acc):
        m_new = jnp.maximum(m_run, jnp.max(sc, axis=-1, keepdims=True).astype(F32))
        pp = jnp.exp2(sc - m_new.astype(BF16))
        acc = jnp.exp2(m_run - m_new) * acc + jnp.dot(pp, vsa_ref[ck], preferred_element_type=F32)
        return m_new, acc

    last = (i0 * Q_BLOCK) // KEY_CHUNK

    def quad_body(kk, carry):
        for u in range(2):
            scores(4 * kk + 2 * u + 1, 1)
            carry = absorb(sc_ref[0], 4 * kk + 2 * u, *carry)
            scores(4 * kk + 2 * u + 2, 0)
            carry = absorb(sc_ref[1], 4 * kk + 2 * u + 1, *carry)
        return carry

    def pair_body(_, carry):
        base = 4 * (last // 4)
        scores(base + 1, 1)
        carry = absorb(sc_ref[0], base, *carry)
        scores(base + 2, 0)
        return absorb(sc_ref[1], base + 1, *carry)

    def odd_body(_, carry):
        scores(last, 1)
        return absorb(sc_ref[0], last - 1, *carry)

    scores(0, 0)
    carry = (jnp.full((rows, 1), -jnp.inf, F32), jnp.zeros((rows, LANES), F32))
    carry = lax.fori_loop(0, last // 4, quad_body, carry)
    carry = lax.fori_loop(0, (last % 4) // 2, pair_body, carry)
    carry = lax.fori_loop(0, last % 2, odd_body, carry)
    _, acc_s = absorb(sc_ref[last % 2] + dmask_ref[...], last, *carry)
    o_sel = acc_s * (1.0 / acc_s[:, NSA_HEAD_DIM:NSA_HEAD_DIM + 1])

    o = part_ref[...] + gsel_ref[...] * o_sel
    lane_lo = lax.broadcasted_iota(jnp.int32, (1, LANES), 1) < NSA_HEAD_DIM
    for qb in range(n_qb):
        heads = [o[qb * rows_qb + g * Q_BLOCK:qb * rows_qb + (g + 1) * Q_BLOCK] for g in range(NSA_GROUP)]
        pairs = [jnp.where(lane_lo, heads[2 * k], pltpu.roll(heads[2 * k + 1], NSA_HEAD_DIM, axis=1))
                 for k in range(NSA_GROUP // 2)]
        o_ref[qb * Q_BLOCK:(qb + 1) * Q_BLOCK, :] = jnp.concatenate(pairs, axis=1).astype(o_ref.dtype)


def _nsa_attention(qc, qr, gates_raw, kc_t, vc_aug, ks_aug, vs_aug, kw_t, vw_aug, imp_mat, n_qb=4):
    bsz, nq, hk, rows_qb, _ = qc.shape
    s_len = nq * Q_BLOCK
    n_ch = ks_aug.shape[2]
    n_units = kw_t.shape[2]
    n_cmp = kc_t.shape[3]
    assert nq % n_qb == 0 and KEY_CHUNK == n_qb * Q_BLOCK
    r = np.arange(n_qb * rows_qb)
    own = (r // rows_qb) * Q_BLOCK + r % Q_BLOCK
    dmask = jnp.asarray(np.where(np.arange(KEY_CHUNK)[None, :] <= own[:, None], 0.0, -np.inf), dtype=BF16)
    qspec = pl.BlockSpec((None, n_qb, None, rows_qb, LANES), lambda b, h, i: (b, i, h, 0, 0))
    per_head = lambda shp: pl.BlockSpec((None, None) + shp, lambda b, h, i: (b, h) + (0,) * len(shp))
    return pl.pallas_call(
        functools.partial(_nsa_kernel, n_qb=n_qb),
        grid=(bsz, hk, nq // n_qb),
        in_specs=[qspec, qspec,
                  pl.BlockSpec((None, n_qb * Q_BLOCK, LANES), lambda b, h, i: (b, i, h)),
                  per_head((LANES, n_cmp)),
                  per_head((n_cmp, LANES)),
                  per_head((n_ch, 2 * LANES, KEY_CHUNK)),
                  per_head((n_ch, KEY_CHUNK, LANES)),
                  per_head((n_units, LANES, Q_BLOCK)),
                  per_head((n_units, Q_BLOCK, LANES)),
                  pl.BlockSpec(imp_mat.shape, lambda b, h, i: (0, 0)),
                  pl.BlockSpec(dmask.shape, lambda b, h, i: (0, 0))],
        out_specs=pl.BlockSpec((None, n_qb * Q_BLOCK, NSA_GROUP * NSA_HEAD_DIM), lambda b, h, i: (b, i, h)),
        out_shape=jax.ShapeDtypeStruct((bsz, s_len, hk * NSA_GROUP * NSA_HEAD_DIM), BF16),
        scratch_shapes=[pltpu.VMEM((2, n_qb * rows_qb, KEY_CHUNK), BF16),
                        pltpu.VMEM((n_qb * rows_qb, LANES), F32),
                        pltpu.VMEM((n_qb * rows_qb, LANES), F32)],
        compiler_params=_params("parallel", "parallel", "arbitrary"),
        name="nsa_attention",
    )(qc, qr, gates_raw, kc_t, vc_aug, ks_aug, vs_aug, kw_t, vw_aug, imp_mat, dmask)


def kernel(x, c, mod_w, mod_b, norm_pre_mix, norm_post_mix, norm_pre_ffn, norm_post_ffn, ffn_w_in, ffn_w_out, ssm_w_in, ssm_conv_w, ssm_conv_b, ssm_dt_bias, ssm_a_log, ssm_d, ssm_norm, ssm_w_out, kv_norm, kv_mod_w, kv_mod_b, w_kv, cmp_pos_k, cmp_w1_k, cmp_w2_k, cmp_pos_v, cmp_w1_v, cmp_w2_v, nsa_w_q, nsa_w_o):
    bsz, s_len, d = x.shape
    depth = mod_w.shape[0]
    n_a = ssm_w_in.shape[0]
    d_ff = ffn_w_out.shape[1]
    d_inner = ssm_w_out.shape[1]
    n_ssm_heads = d_inner // SSM_HEADDIM
    d_xbc = ssm_conv_w.shape[-1]
    dq = NSA_HEADS * NSA_HEAD_DIM
    dkv = NSA_KV_HEADS * NSA_HEAD_DIM
    assert s_len % KEY_CHUNK == 0 and s_len // SEL_BLOCK <= LANES and s_len // SEL_BLOCK >= N_SEL

    mods = _mods(c, mod_w, mod_b)
    kv_mods = _mods(c, kv_mod_w[None], kv_mod_b[None])[0]
    vec = lambda m, k: m[:, k * d:(k + 1) * d].reshape(bsz, 1, d)

    shared = None
    tables = None
    for i in range(depth):
        sh_m, sc_m, g_m, sh_f, sc_f, g_f = [vec(mods[i], k) for k in range(6)]
        if i == n_a:
            tables = _rope_tables(s_len)
            w_kv_b = w_kv.astype(BF16)
            kc, vc, ks, vs, kw, vw = _adaln_mm(x, kv_norm, vec(kv_mods, 0), vec(kv_mods, 1),
                                               [w_kv_b[:, k * dkv:(k + 1) * dkv] for k in range(6)])
            kc_t = _compress(kc, cmp_pos_k, cmp_w1_k, cmp_w2_k, transpose_out=True)
            vc_aug = _compress(vc, cmp_pos_v, cmp_w1_v, cmp_w2_v, transpose_out=False)
            shared = (kc_t, vc_aug) + _kv_post(ks, vs, kw, vw, tables)
            imp_mat = jnp.asarray(_importance_matrix(s_len // CMP_STRIDE, LANES), dtype=BF16)
        if i < n_a:
            w_in = ssm_w_in[i].astype(BF16)
            w_dt = jnp.pad(w_in[:, d_inner + d_xbc:], ((0, 0), (0, LANES - n_ssm_heads)))
            z, xbc, dt_raw = _adaln_mm(x, norm_pre_mix[i], sh_m, sc_m,
                                       [w_in[:, :d_inner], w_in[:, d_inner:d_inner + d_xbc], w_dt])
            y = _ssd(xbc, z, dt_raw, ssm_conv_w[i], ssm_conv_b[i], ssm_dt_bias[i], ssm_a_log[i],
                     ssm_d[i], ssm_norm[i])
            x = _mm_post(y, ssm_w_out[i].astype(BF16), x, g_m, norm_post_mix[i])
        else:
            w_q = nsa_w_q[i - n_a].astype(BF16)
            w_g = w_q[:, dq:].reshape(d, NSA_KV_HEADS, NSA_GROUP * N_BRANCH)
            w_g = jnp.pad(w_g, ((0, 0), (0, 0), (0, LANES - NSA_GROUP * N_BRANCH))).reshape(d, NSA_KV_HEADS * LANES)
            qc, qr, gates_raw = _q_proj(x, norm_pre_mix[i], sh_m, sc_m, w_q[:, :dq], w_g, tables)
            o = _nsa_attention(qc, qr, gates_raw, *shared, imp_mat)
            x = _mm_post(o, nsa_w_o[i - n_a].astype(BF16), x, g_m, norm_post_mix[i])
        w_ffn = ffn_w_in[i].astype(BF16)
        x = _ffn(x, norm_pre_ffn[i], sh_f, sc_f, w_ffn[:, :d_ff], w_ffn[:, d_ff:], ffn_w_out[i].astype(BF16),
                 norm_post_ffn[i], g_f)
    return x
```
